```python
import jax, jax.numpy as jnp
from jax import lax
import numpy as np

D_MODEL = 1024
BATCH = 16
SEQ = 2048
DEPTH = 2
DEC_BATCH = 4
DEC_SEQ = 4096
PAST_LEN = 128

GRID_W = 64
N_MEM = 256
HEAD_DIM = 64
N_HEADS = 8
N_KV_HEADS = 2
ATTN_W = N_HEADS * HEAD_DIM
KV_W = N_KV_HEADS * HEAD_DIM
POOL_W = 256
POOL_WINDOWS = (2, 4, 8, 16)
POOL_GROUP = POOL_W // len(POOL_WINDOWS)
N_XHEADS = 4
XATTN_W = N_XHEADS * HEAD_DIM
MIX_W = POOL_W + ATTN_W + XATTN_W
IN_W = 2 * POOL_W + 2 * ATTN_W + 2 * KV_W + 2 * XATTN_W
Q_BLOCK = 128
ROPE_THETA = 10000.0
ROPE_PAIRS = HEAD_DIM // 4
EPS = 1e-6

kernel_name = "hybrid_pool_gqa_memory_encoder"


def rms_norm(x, g):
    xf = x.astype(jnp.float32)
    y = xf * lax.rsqrt(jnp.mean(xf * xf, axis=-1, keepdims=True) + EPS)
    return (y * g.astype(jnp.float32)).astype(x.dtype)


def axial_rope(T):
    rows = T // GRID_W
    row = jnp.repeat(jnp.arange(rows), GRID_W).astype(jnp.float32)
    col = jnp.tile(jnp.arange(GRID_W), rows).astype(jnp.float32)
    freqs = ROPE_THETA ** (-jnp.arange(ROPE_PAIRS, dtype=jnp.float32) / ROPE_PAIRS)
    ang = jnp.stack([row[:, None] * freqs, col[:, None] * freqs], axis=1)
    return jnp.cos(ang), jnp.sin(ang)


def apply_rope(x, cos, sin):
    B, T, H, D = x.shape
    xr = x.astype(jnp.float32).reshape(B, T, H, 2, 2, ROPE_PAIRS)
    a, b = xr[..., 0, :], xr[..., 1, :]
    c, s = cos[None, :, None], sin[None, :, None]
    out = jnp.stack([a * c - b * s, b * c + a * s], axis=-2)
    return out.reshape(B, T, H, D).astype(x.dtype)


def multiscale_pool(u, pool_w, pool_scale):
    B, T, _ = u.shape
    uf = u.astype(jnp.float32)
    cs = jnp.concatenate([jnp.zeros((B, 1, POOL_W), jnp.float32), jnp.cumsum(uf, axis=1)], axis=1)
    t = jnp.arange(T)
    outs = []
    for g, w in enumerate(POOL_WINDOWS):
        lo = jnp.clip(t - w // 2, 0, T)
        hi = jnp.clip(t + w - w // 2, 0, T)
        sl = slice(g * POOL_GROUP, (g + 1) * POOL_GROUP)
        csg = cs[:, :, sl]
        cnt = (hi - lo).astype(jnp.float32)[None, :, None]
        mean = (jnp.take(csg, hi, axis=1) - jnp.take(csg, lo, axis=1)) / cnt
        d = (mean - uf[:, :, sl]).astype(u.dtype)
        outs.append(jnp.einsum('btc,cd->btd', d, pool_w[g]))
    return jnp.concatenate(outs, axis=-1) * pool_scale


def block_self_attention(q, k, v):
    B, T, H, D = q.shape
    G = H // N_KV_HEADS
    nb = T // Q_BLOCK
    qb = q.reshape(B, nb, Q_BLOCK, N_KV_HEADS, G, D).transpose(1, 0, 2, 3, 4, 5)
    scale = D ** -0.5

    def one_block(qblk):
        s = jnp.einsum('bqkgd,bskd->bkgqs', qblk, k, preferred_element_type=jnp.float32) * scale
        p = jax.nn.softmax(s, axis=-1)
        return jnp.einsum('bkgqs,bskd->bqkgd', p.astype(v.dtype), v)

    o = lax.map(one_block, qb)
    return o.transpose(1, 0, 2, 3, 4, 5).reshape(B, T, H * D)


def memory_attention(q, k, v):
    B, T, XH, D = q.shape
    s = jnp.einsum('bthd,bmhd->bhtm', q, k, preferred_element_type=jnp.float32) * (D ** -0.5)
    p = jax.nn.softmax(s, axis=-1)
    return jnp.einsum('bhtm,bmhd->bthd', p.astype(v.dtype), v).reshape(B, T, XH * D)


SPLITS = [int(i) for i in np.cumsum([POOL_W, POOL_W, ATTN_W, KV_W, KV_W, ATTN_W, XATTN_W])]


def hybrid_layer(x, mem, cos, sin, norm_pre, norm_post, w_in, pool_w, pool_scale,
                 q_norm, k_norm, mem_norm, w_mem_kv, w_out):
    B, T, _ = x.shape
    h = rms_norm(x, norm_pre)
    z = jnp.einsum('btd,de->bte', h, w_in)
    u_pool, g_pool, q, k, v, g_attn, q_x, g_x = jnp.split(z, SPLITS, axis=-1)

    pool_out = multiscale_pool(u_pool, pool_w, pool_scale) * jax.nn.silu(g_pool)

    q = apply_rope(rms_norm(q.reshape(B, T, N_HEADS, HEAD_DIM), q_norm), cos, sin)
    k = apply_rope(rms_norm(k.reshape(B, T, N_KV_HEADS, HEAD_DIM), k_norm), cos, sin)
    v = v.reshape(B, T, N_KV_HEADS, HEAD_DIM)
    attn_out = block_self_attention(q, k, v) * jax.nn.silu(g_attn)

    mh = rms_norm(mem, mem_norm)
    kv_m = jnp.einsum('bmd,de->bme', mh, w_mem_kv)
    k_m, v_m = jnp.split(kv_m, [XATTN_W], axis=-1)
    M = mem.shape[1]
    x_out = memory_attention(q_x.reshape(B, T, N_XHEADS, HEAD_DIM),
                             k_m.reshape(B, M, N_XHEADS, HEAD_DIM),
                             v_m.reshape(B, M, N_XHEADS, HEAD_DIM)) * jax.nn.silu(g_x)

    mix = jnp.concatenate([pool_out, attn_out, x_out], axis=-1)
    y = jnp.einsum('bte,ed->btd', mix, w_out)
    return x + rms_norm(y, norm_post)


def run_trunk(x, mem, norm_pre, norm_post, w_in, pool_w, pool_scale,
              q_norm, k_norm, mem_norm, w_mem_kv, w_out):
    cos, sin = axial_rope(x.shape[1])
    for l in range(DEPTH):
        x = hybrid_layer(x, mem, cos, sin, norm_pre[l], norm_post[l], w_in[l], pool_w[l], pool_scale[l],
                         q_norm[l], k_norm[l], mem_norm[l], w_mem_kv[l], w_out[l])
    return x


def setup_inputs(seed: int = 0) -> dict:
    key = jax.random.key(seed)
    ks = jax.random.split(key, 16)
    f32 = jnp.float32

    def gain(k, shape):
        return 1.0 + 0.02 * jax.random.normal(k, shape, f32)

    return {
        "x_prompt": jax.random.normal(ks[0], (BATCH, SEQ, D_MODEL), f32),
        "x_sample": jax.random.normal(ks[1], (DEC_BATCH, DEC_SEQ, D_MODEL), f32),
        "mem_prompt": jax.random.normal(ks[2], (BATCH, N_MEM, D_MODEL), f32),
        "mem_sample": jax.random.normal(ks[3], (DEC_BATCH, N_MEM, D_MODEL), f32),
        "norm_pre": gain(ks[4], (DEPTH, D_MODEL)),
        "norm_post": gain(ks[5], (DEPTH, D_MODEL)),
        "w_in": jax.random.normal(ks[6], (DEPTH, D_MODEL, IN_W), f32) * D_MODEL ** -0.5,
        "pool_w": jax.random.normal(ks[7], (DEPTH, len(POOL_WINDOWS), POOL_GROUP, POOL_GROUP), f32) * POOL_GROUP ** -0.5,
        "pool_scale": 1.0 + 0.1 * jax.random.normal(ks[8], (DEPTH, POOL_W), f32),
        "q_norm": gain(ks[9], (DEPTH, HEAD_DIM)),
        "k_norm": gain(ks[10], (DEPTH, HEAD_DIM)),
        "mem_norm": gain(ks[11], (DEPTH, D_MODEL)),
        "w_mem_kv": jax.random.normal(ks[12], (DEPTH, D_MODEL, 2 * XATTN_W), f32) * D_MODEL ** -0.5,
        "w_out": jax.random.normal(ks[13], (DEPTH, MIX_W, D_MODEL), f32) * MIX_W ** -0.5,
    }


def reference(x_prompt, x_sample, mem_prompt, mem_sample, norm_pre, norm_post, w_in, pool_w, pool_scale,
              q_norm, k_norm, mem_norm, w_mem_kv, w_out):
    y_prompt = run_trunk(x_prompt, mem_prompt, norm_pre, norm_post, w_in, pool_w, pool_scale,
                         q_norm, k_norm, mem_norm, w_mem_kv, w_out)
    y_sample = run_trunk(x_sample, mem_sample, norm_pre, norm_post, w_in, pool_w, pool_scale,
                         q_norm, k_norm, mem_norm, w_mem_kv, w_out)
    return (y_prompt, y_sample)
```

```python
import functools

import jax
import jax.numpy as jnp
from jax import lax
from jax.experimental import pallas as pl
from jax.experimental.pallas import tpu as pltpu

D_MODEL = 1024
DEPTH = 2
GRID_W = 64
N_MEM = 256
HEAD_DIM = 64
N_HEADS = 8
N_KV_HEADS = 2
ATTN_W = N_HEADS * HEAD_DIM
KV_W = N_KV_HEADS * HEAD_DIM
POOL_W = 256
POOL_WINDOWS = (2, 4, 8, 16)
POOL_GROUP = POOL_W // len(POOL_WINDOWS)
N_XHEADS = 4
XATTN_W = N_XHEADS * HEAD_DIM
MIX_W = POOL_W + ATTN_W + XATTN_W
IN_W = 2 * POOL_W + 2 * ATTN_W + 2 * KV_W + 2 * XATTN_W
ROPE_THETA = 10000.0
ROPE_PAIRS = HEAD_DIM // 4
EPS = 1e-6

LANES = 128
POOL_HALO = 8
PROJ_ROWS = 512
Q_ROWS = 256
VMEM_LIMIT_BYTES = 56 * 1024 * 1024

_OFF_U = 0
_OFF_GP = POOL_W
_OFF_Q = 2 * POOL_W
_OFF_K = _OFF_Q + ATTN_W
_OFF_V = _OFF_K + KV_W
_OFF_GA = _OFF_V + KV_W
_OFF_QX = _OFF_GA + ATTN_W
_OFF_GX = _OFF_QX + XATTN_W

_BF16 = jnp.bfloat16
_F32 = jnp.float32


def _dot(a, b):
    return jnp.dot(a, b, preferred_element_type=_F32)


def _silu(g):
    return g * (1.0 / (1.0 + jnp.exp(-g)))


def _lane_is_even_head(shape):
    return (lax.broadcasted_iota(jnp.int32, shape, len(shape) - 1) % LANES) < HEAD_DIM


def _mem_kv_kernel(mem_ref, g_ref, w_ref, kt_ref, v_ref):
    m = mem_ref[0]
    ms = jnp.mean(m * m, axis=-1, keepdims=True)
    mh = (m * lax.rsqrt(ms + EPS) * g_ref[...]).astype(_BF16)
    kv = _dot(mh, w_ref[...])
    k_t = kv[:, :XATTN_W].T
    v = kv[:, XATTN_W:]
    top = lax.broadcasted_iota(jnp.int32, (LANES, N_MEM), 0) < HEAD_DIM
    left = _lane_is_even_head((N_MEM, LANES))
    for pair in range(N_XHEADS // 2):
        kp = k_t[pair * LANES:(pair + 1) * LANES]
        vp = v[:, pair * LANES:(pair + 1) * LANES]
        kt_ref[0, 2 * pair, 0] = jnp.where(top, kp, 0.0).astype(_BF16)
        kt_ref[0, 2 * pair + 1, 0] = jnp.where(top, 0.0, kp).astype(_BF16)
        v_ref[0, 2 * pair] = jnp.where(left, vp, 0.0).astype(_BF16)
        v_ref[0, 2 * pair + 1] = jnp.where(left, 0.0, vp).astype(_BF16)


def _mem_kv(mem, g, w):
    b = mem.shape[0]
    return pl.pallas_call(
        _mem_kv_kernel,
        grid=(b,),
        in_specs=[
            pl.BlockSpec((1, N_MEM, D_MODEL), lambda i: (i, 0, 0)),
            pl.BlockSpec((1, D_MODEL), lambda i: (0, 0)),
            pl.BlockSpec((D_MODEL, 2 * XATTN_W), lambda i: (0, 0)),
        ],
        out_specs=[
            pl.BlockSpec((1, N_XHEADS, 1, LANES, N_MEM), lambda i: (i, 0, 0, 0, 0)),
            pl.BlockSpec((1, N_XHEADS, N_MEM, LANES), lambda i: (i, 0, 0, 0)),
        ],
        out_shape=[
            jax.ShapeDtypeStruct((b, N_XHEADS, 1, LANES, N_MEM), _BF16),
            jax.ShapeDtypeStruct((b, N_XHEADS, N_MEM, LANES), _BF16),
        ],
        compiler_params=pltpu.CompilerParams(
            dimension_semantics=("arbitrary",), vmem_limit_bytes=VMEM_LIMIT_BYTES),
        name="mem_kv",
    )(mem, g, w)


def _swap16(x):
    n = x.shape[-1]
    lane = lax.broadcasted_iota(jnp.int32, x.shape, 1)
    first = (lane & ROPE_PAIRS) == 0
    return jnp.where(first, pltpu.roll(x, n - ROPE_PAIRS, 1), pltpu.roll(x, ROPE_PAIRS, 1))


def _norm_rope(z, seg, cos_tab, sin_tab):
    ssq = _dot((z * z).astype(_BF16), seg)
    r = lax.rsqrt(ssq * (1.0 / HEAD_DIM) + EPS)
    return r * (z * cos_tab + _swap16(z) * sin_tab)


def _proj_kernel(x_ref, g_ref, w_ref, seg_ref, cq_ref, sq_ref, ck_ref, sk_ref,
                 u_ref, gate_ref, q_ref, qx_ref, kt_ref, v_ref):
    x = x_ref[0]
    ms = jnp.mean(x * x, axis=-1, keepdims=True)
    h = (x * lax.rsqrt(ms + EPS) * g_ref[...]).astype(_BF16)
    z = _dot(h, w_ref[...])

    u_ref[0] = z[:, _OFF_U:_OFF_U + POOL_W]
    gate_ref[0, :, 0:POOL_W] = _silu(z[:, _OFF_GP:_OFF_GP + POOL_W]).astype(_BF16)
    gate_ref[0, :, POOL_W:POOL_W + ATTN_W] = _silu(z[:, _OFF_GA:_OFF_GA + ATTN_W]).astype(_BF16)
    gate_ref[0, :, POOL_W + ATTN_W:MIX_W] = _silu(z[:, _OFF_GX:_OFF_GX + XATTN_W]).astype(_BF16)
    qx_ref[0] = (z[:, _OFF_QX:_OFF_QX + XATTN_W] * (HEAD_DIM ** -0.5)).astype(_BF16)

    seg = seg_ref[...]
    cq, sq = cq_ref[...], sq_ref[...]
    for pair in range(N_HEADS // 2):
        lo = _OFF_Q + pair * LANES
        q = _norm_rope(z[:, lo:lo + LANES], seg, cq, sq)
        q_ref[0, :, pair * LANES:(pair + 1) * LANES] = q.astype(_BF16)

    k = _norm_rope(z[:, _OFF_K:_OFF_K + KV_W], seg, ck_ref[...], sk_ref[...])
    k_t = k.T
    k_t_sw = pltpu.roll(k_t, HEAD_DIM, 0)
    top = lax.broadcasted_iota(jnp.int32, k_t.shape, 0) < HEAD_DIM
    kt_ref[0, 0, 0] = jnp.where(top, k_t, 0.0).astype(_BF16)
    kt_ref[0, 1, 0] = jnp.where(top, 0.0, k_t_sw).astype(_BF16)
    kt_ref[0, 2, 0] = jnp.where(top, k_t_sw, 0.0).astype(_BF16)
    kt_ref[0, 3, 0] = jnp.where(top, 0.0, k_t).astype(_BF16)

    v = z[:, _OFF_V:_OFF_V + KV_W]
    v_sw = pltpu.roll(v, HEAD_DIM, 1)
    left = _lane_is_even_head(v.shape)
    v_ref[0, 0] = jnp.where(left, v, 0.0).astype(_BF16)
    v_ref[0, 1] = jnp.where(left, 0.0, v_sw).astype(_BF16)
    v_ref[0, 2] = jnp.where(left, v_sw, 0.0).astype(_BF16)
    v_ref[0, 3] = jnp.where(left, 0.0, v).astype(_BF16)


def _proj(x, g, w, seg, cq, sq, ck, sk):
    b, t, _ = x.shape
    rows = PROJ_ROWS
    n = t // rows
    tab = pl.BlockSpec((rows, LANES), lambda i, j: (j, 0))
    return pl.pallas_call(
        _proj_kernel,
        grid=(b, n),
        in_specs=[
            pl.BlockSpec((1, rows, D_MODEL), lambda i, j: (i, j, 0)),
            pl.BlockSpec((1, D_MODEL), lambda i, j: (0, 0)),
            pl.BlockSpec((D_MODEL, IN_W), lambda i, j: (0, 0)),
            pl.BlockSpec((LANES, LANES), lambda i, j: (0, 0)),
            tab, tab, tab, tab,
        ],
        out_specs=[
            pl.BlockSpec((1, rows, POOL_W), lambda i, j: (i, j, 0)),
            pl.BlockSpec((1, rows, MIX_W), lambda i, j: (i, j, 0)),
            pl.BlockSpec((1, rows, ATTN_W), lambda i, j: (i, j, 0)),
            pl.BlockSpec((1, rows, XATTN_W), lambda i, j: (i, j, 0)),
            pl.BlockSpec((1, 4, 1, KV_W, rows), lambda i, j: (i, 0, j, 0, 0)),
            pl.BlockSpec((1, 4, rows, KV_W), lambda i, j: (i, 0, j, 0)),
        ],
        out_shape=[
            jax.ShapeDtypeStruct((b, t, POOL_W), _F32),
            jax.ShapeDtypeStruct((b, t, MIX_W), _BF16),
            jax.ShapeDtypeStruct((b, t, ATTN_W), _BF16),
            jax.ShapeDtypeStruct((b, t, XATTN_W), _BF16),
            jax.ShapeDtypeStruct((b, 4, n, KV_W, rows), _BF16),
            jax.ShapeDtypeStruct((b, 4, t, KV_W), _BF16),
        ],
        compiler_params=pltpu.CompilerParams(
            dimension_semantics=("arbitrary", "arbitrary"), vmem_limit_bytes=VMEM_LIMIT_BYTES),
        name="proj",
    )(x, g, w, seg, cq, sq, ck, sk)


def _attend_pair(q_pair, kt_ref, v_ref, slot, n_chunks, chunk):
    tq = q_pair.shape[0]
    left_k = _lane_is_even_head((chunk, LANES))
    ones_e = jnp.where(left_k, 1.0, 0.0).astype(_BF16)
    ones_o = jnp.where(left_k, 0.0, 1.0).astype(_BF16)
    left_acc = _lane_is_even_head((tq, 2 * LANES))

    def body(c, carry):
        m_e, m_o, acc = carry
        rows = pl.ds(0, chunk) if n_chunks == 1 else pl.ds(pl.multiple_of(c * chunk, chunk), chunk)
        s_e = _dot(q_pair, kt_ref[slot, c])
        s_o = _dot(q_pair, kt_ref[slot + 1, c])
        mn_e = jnp.maximum(m_e, jnp.max(s_e, axis=-1, keepdims=True))
        mn_o = jnp.maximum(m_o, jnp.max(s_o, axis=-1, keepdims=True))
        p_e = jnp.exp(s_e - mn_e).astype(_BF16)
        p_o = jnp.exp(s_o - mn_o).astype(_BF16)
        alpha = jnp.where(left_acc, jnp.exp(m_e - mn_e), jnp.exp(m_o - mn_o))
        rhs_e = jnp.concatenate([v_ref[slot, rows, :], ones_e], axis=1)
        rhs_o = jnp.concatenate([v_ref[slot + 1, rows, :], ones_o], axis=1)
        acc = acc * alpha + _dot(p_e, rhs_e) + _dot(p_o, rhs_o)
        return mn_e, mn_o, acc

    m0 = jnp.full((tq, 1), -jnp.inf, _F32)
    acc0 = jnp.zeros((tq, 2 * LANES), _F32)
    if n_chunks == 1:
        _, _, acc = body(0, (m0, m0, acc0))
    else:
        _, _, acc = lax.fori_loop(0, n_chunks, body, (m0, m0, acc0))
    return acc[:, :LANES] / acc[:, LANES:]


def _pool_delta(u_ref, up_ref, un_ref, seq_len):
    tq = u_ref.shape[1]
    n_ext = tq + 2 * POOL_HALO
    u = u_ref[0]
    ext = jnp.concatenate([up_ref[0], u, un_ref[0]], axis=0)
    t0 = pl.program_id(1) * tq
    t_ext = t0 - POOL_HALO + lax.broadcasted_iota(jnp.int32, (n_ext, 1), 0)
    ext = jnp.where((t_ext >= 0) & (t_ext < seq_len), ext, 0.0)
    sums = {}
    run = ext
    for w in POOL_WINDOWS:
        run = run + pltpu.roll(run, n_ext - w // 2, 0)
        if w // 2 == POOL_HALO:
            sums[w] = run[0:tq]
        else:
            sums[w] = pltpu.roll(run, w // 2, 0)[POOL_HALO:POOL_HALO + tq]
    t = t0 + lax.broadcasted_iota(jnp.int32, (tq, 1), 0)
    inv = {}
    for w in POOL_WINDOWS:
        cnt = jnp.minimum(t + (w - w // 2), seq_len) - jnp.maximum(t - w // 2, 0)
        inv[w] = 1.0 / cnt.astype(_F32)
    left = _lane_is_even_head((tq, LANES))
    w0, w1, w2, w3 = POOL_WINDOWS
    mean_lo = jnp.where(left, sums[w0][:, :LANES] * inv[w0], sums[w1][:, :LANES] * inv[w1])
    mean_hi = jnp.where(left, sums[w2][:, LANES:] * inv[w2], sums[w3][:, LANES:] * inv[w3])
    return jnp.concatenate([mean_lo, mean_hi], axis=1) - u


def _mix_kernel(q_ref, qx_ref, kt_ref, v_ref, kmt_ref, vm_ref, gate_ref, u_ref, up_ref, un_ref,
                x_ref, wpool_ref, pscale_ref, wout_ref, gpost_ref, y_ref, mix_ref, *, seq_len):
    n_chunks = kt_ref.shape[2]
    chunk = kt_ref.shape[4]

    d = _pool_delta(u_ref, up_ref, un_ref, seq_len).astype(_BF16)
    pool = _dot(d, wpool_ref[...]) * pscale_ref[...]
    mix_ref[:, 0:POOL_W] = (pool * gate_ref[0, :, 0:POOL_W].astype(_F32)).astype(_BF16)

    for pair in range(N_HEADS // 2):
        group = pair // (N_HEADS // N_KV_HEADS // 2)
        lo = pair * LANES
        out = _attend_pair(q_ref[0, :, lo:lo + LANES], kt_ref.at[0], v_ref.at[0], 2 * group,
                           n_chunks, chunk)
        c0 = POOL_W + lo
        mix_ref[:, c0:c0 + LANES] = (out * gate_ref[0, :, c0:c0 + LANES].astype(_F32)).astype(_BF16)

    for pair in range(N_XHEADS // 2):
        lo = pair * LANES
        out = _attend_pair(qx_ref[0, :, lo:lo + LANES], kmt_ref.at[0], vm_ref.at[0], 2 * pair,
                           1, N_MEM)
        c0 = POOL_W + ATTN_W + lo
        mix_ref[:, c0:c0 + LANES] = (out * gate_ref[0, :, c0:c0 + LANES].astype(_F32)).astype(_BF16)

    y = _dot(mix_ref[...], wout_ref[...])
    ms = jnp.mean(y * y, axis=-1, keepdims=True)
    y_ref[0] = x_ref[0] + y * lax.rsqrt(ms + EPS) * gpost_ref[...]


def _mix(q, qx, kt, v, kmt, vm, gate, u, x, wpool, pscale, wout, gpost):
    b, t, _ = x.shape
    tq = Q_ROWS
    n = t // tq
    n_chunks, chunk = kt.shape[2], kt.shape[4]
    halo_per_q = tq // POOL_HALO
    n_halo = t // POOL_HALO
    return pl.pallas_call(
        functools.partial(_mix_kernel, seq_len=t),
        grid=(b, n),
        in_specs=[
            pl.BlockSpec((1, tq, ATTN_W), lambda i, j: (i, j, 0)),
            pl.BlockSpec((1, tq, XATTN_W), lambda i, j: (i, j, 0)),
            pl.BlockSpec((1, 4, n_chunks, KV_W, chunk), lambda i, j: (i, 0, 0, 0, 0)),
            pl.BlockSpec((1, 4, t, KV_W), lambda i, j: (i, 0, 0, 0)),
            pl.BlockSpec((1, N_XHEADS, 1, LANES, N_MEM), lambda i, j: (i, 0, 0, 0, 0)),
            pl.BlockSpec((1, N_XHEADS, N_MEM, LANES), lambda i, j: (i, 0, 0, 0)),
            pl.BlockSpec((1, tq, MIX_W), lambda i, j: (i, j, 0)),
            pl.BlockSpec((1, tq, POOL_W), lambda i, j: (i, j, 0)),
            pl.BlockSpec((1, POOL_HALO, POOL_W),
                         lambda i, j: (i, jnp.maximum(j * halo_per_q - 1, 0), 0)),
            pl.BlockSpec((1, POOL_HALO, POOL_W),
                         lambda i, j: (i, jnp.minimum((j + 1) * halo_per_q, n_halo - 1), 0)),
            pl.BlockSpec((1, tq, D_MODEL), lambda i, j: (i, j, 0)),
            pl.BlockSpec((POOL_W, POOL_W), lambda i, j: (0, 0)),
            pl.BlockSpec((1, POOL_W), lambda i, j: (0, 0)),
            pl.BlockSpec((MIX_W, D_MODEL), lambda i, j: (0, 0)),
            pl.BlockSpec((1, D_MODEL), lambda i, j: (0, 0)),
        ],
        out_specs=pl.BlockSpec((1, tq, D_MODEL), lambda i, j: (i, j, 0)),
        out_shape=jax.ShapeDtypeStruct((b, t, D_MODEL), _F32),
        scratch_shapes=[pltpu.VMEM((tq, MIX_W), _BF16)],
        compiler_params=pltpu.CompilerParams(
            dimension_semantics=("arbitrary", "arbitrary"), vmem_limit_bytes=VMEM_LIMIT_BYTES),
        name="mix",
    )(q, qx, kt, v, kmt, vm, gate, u, u, u, x, wpool, pscale, wout, gpost)


def _rope_tables(t, gain, scale):
    rows = t // GRID_W
    row = jnp.repeat(jnp.arange(rows), GRID_W).astype(_F32)
    col = jnp.tile(jnp.arange(GRID_W), rows).astype(_F32)
    freqs = ROPE_THETA ** (-jnp.arange(ROPE_PAIRS, dtype=_F32) / ROPE_PAIRS)
    ang = jnp.stack([row[:, None] * freqs, col[:, None] * freqs], axis=1)
    cos, sin = jnp.cos(ang), jnp.sin(ang)
    cos_h = jnp.stack([cos, cos], axis=2).reshape(t, HEAD_DIM)
    sin_h = jnp.stack([-sin, sin], axis=2).reshape(t, HEAD_DIM)
    g = gain.astype(_F32)
    g_sw = g.reshape(2, 2, ROPE_PAIRS)[:, ::-1, :].reshape(HEAD_DIM)
    cos_h = cos_h * (g * scale)
    sin_h = sin_h * (g_sw * scale)
    return jnp.tile(cos_h, (1, LANES // HEAD_DIM)), jnp.tile(sin_h, (1, LANES // HEAD_DIM))


def _block_diag(blocks):
    n = len(blocks)
    rows = []
    for i, blk in enumerate(blocks):
        rows.append(jnp.concatenate(
            [blk if j == i else jnp.zeros_like(blk) for j in range(n)], axis=1))
    return jnp.concatenate(rows, axis=0)


def _trunk(x, mem, norm_pre, norm_post, w_in, pool_w, pool_scale, q_norm, k_norm, mem_norm,
           w_mem_kv, w_out):
    t = x.shape[1]
    ones = jnp.ones((HEAD_DIM, HEAD_DIM), _BF16)
    seg = _block_diag([ones, ones])
    for l in range(DEPTH):
        cq, sq = _rope_tables(t, q_norm[l], HEAD_DIM ** -0.5)
        ck, sk = _rope_tables(t, k_norm[l], 1.0)
        kmt, vm = _mem_kv(mem, mem_norm[l][None], w_mem_kv[l].astype(_BF16))
        u, gate, q, qx, kt, v = _proj(x, norm_pre[l][None], w_in[l].astype(_BF16), seg,
                                      cq, sq, ck, sk)
        wpool = _block_diag([pool_w[l, g] for g in range(len(POOL_WINDOWS))]).astype(_BF16)
        x = _mix(q, qx, kt, v, kmt, vm, gate, u, x, wpool, pool_scale[l][None],
                 w_out[l].astype(_BF16), norm_post[l][None])
    return x


def kernel(x_prompt, x_sample, mem_prompt, mem_sample, norm_pre, norm_post, w_in, pool_w, pool_scale,
           q_norm, k_norm, mem_norm, w_mem_kv, w_out):
    weights = (norm_pre, norm_post, w_in, pool_w, pool_scale, q_norm, k_norm, mem_norm, w_mem_kv, w_out)
    return (_trunk(x_prompt, mem_prompt, *weights), _trunk(x_sample, mem_sample, *weights))
```

```python
import functools
import math

import jax
import jax.numpy as jnp
from jax import lax
from jax.experimental import pallas as pl
from jax.experimental.pallas import tpu as pltpu

D_MODEL = 1024
DEPTH = 2
GRID_W = 64
N_MEM = 256
HEAD_DIM = 64
N_HEADS = 8
N_KV_HEADS = 2
ATTN_W = N_HEADS * HEAD_DIM
KV_W = N_KV_HEADS * HEAD_DIM
POOL_W = 256
POOL_WINDOWS = (2, 4, 8, 16)
POOL_GROUP = POOL_W // len(POOL_WINDOWS)
N_XHEADS = 4
XATTN_W = N_XHEADS * HEAD_DIM
MIX_W = POOL_W + ATTN_W + XATTN_W
IN_W = 2 * POOL_W + 2 * ATTN_W + 2 * KV_W + 2 * XATTN_W
ROPE_THETA = 10000.0
ROPE_PAIRS = HEAD_DIM // 4
EPS = 1e-6
Q_SCALE = math.log2(math.e) * HEAD_DIM ** -0.5

LANES = 128
POOL_HALO = 8
PROJ_ROWS = 512
KEY_CHUNK = 512
Q_ROWS = 256
VMEM_LIMIT_BYTES = 56 * 1024 * 1024

_OFF_U = 0
_OFF_GP = POOL_W
_OFF_Q = 2 * POOL_W
_OFF_K = _OFF_Q + ATTN_W
_OFF_V = _OFF_K + KV_W
_OFF_GA = _OFF_V + KV_W
_OFF_QX = _OFF_GA + ATTN_W
_OFF_GX = _OFF_QX + XATTN_W

_BF16 = jnp.bfloat16
_F32 = jnp.float32


def _dot(a, b):
    return jnp.dot(a, b, preferred_element_type=_F32)


def _silu(g):
    return g * (1.0 / (1.0 + jnp.exp(-g)))


def _lane_is_even_head(shape):
    return (lax.broadcasted_iota(jnp.int32, shape, len(shape) - 1) % LANES) < HEAD_DIM


def _mem_kv_kernel(mem_ref, g_ref, w_ref, kt_ref, v_ref):
    m = mem_ref[0]
    ms = jnp.mean(m * m, axis=-1, keepdims=True)
    mh = (m * lax.rsqrt(ms + EPS) * g_ref[...]).astype(_BF16)
    kv = _dot(mh, w_ref[...])
    k_t = kv[:, :XATTN_W].T
    v = kv[:, XATTN_W:]
    top = lax.broadcasted_iota(jnp.int32, (LANES, N_MEM), 0) < HEAD_DIM
    left = _lane_is_even_head((N_MEM, LANES))
    for pair in range(N_XHEADS // 2):
        kp = k_t[pair * LANES:(pair + 1) * LANES]
        vp = v[:, pair * LANES:(pair + 1) * LANES]
        kt_ref[0, 2 * pair] = jnp.where(top, kp, 0.0).astype(_BF16)
        kt_ref[0, 2 * pair + 1] = jnp.where(top, 0.0, kp).astype(_BF16)
        v_ref[0, 2 * pair] = jnp.where(left, vp, 0.0).astype(_BF16)
        v_ref[0, 2 * pair + 1] = jnp.where(left, 0.0, vp).astype(_BF16)


def _mem_kv(mem, g, w):
    b = mem.shape[0]
    return pl.pallas_call(
        _mem_kv_kernel,
        grid=(b,),
        in_specs=[
            pl.BlockSpec((1, N_MEM, D_MODEL), lambda i: (i, 0, 0)),
            pl.BlockSpec((1, D_MODEL), lambda i: (0, 0)),
            pl.BlockSpec((D_MODEL, 2 * XATTN_W), lambda i: (0, 0)),
        ],
        out_specs=[
            pl.BlockSpec((1, N_XHEADS, LANES, N_MEM), lambda i: (i, 0, 0, 0)),
            pl.BlockSpec((1, N_XHEADS, N_MEM, LANES), lambda i: (i, 0, 0, 0)),
        ],
        out_shape=[
            jax.ShapeDtypeStruct((b, N_XHEADS, LANES, N_MEM), _BF16),
            jax.ShapeDtypeStruct((b, N_XHEADS, N_MEM, LANES), _BF16),
        ],
        compiler_params=pltpu.CompilerParams(
            dimension_semantics=("arbitrary",), vmem_limit_bytes=VMEM_LIMIT_BYTES),
        name="mem_kv",
    )(mem, g, w)


def _swap16(x):
    n = x.shape[-1]
    lane = lax.broadcasted_iota(jnp.int32, x.shape, 1)
    first = (lane & ROPE_PAIRS) == 0
    return jnp.where(first, pltpu.roll(x, n - ROPE_PAIRS, 1), pltpu.roll(x, ROPE_PAIRS, 1))


def _norm_rope(z, seg, cos_tab, sin_tab):
    ssq = _dot((z * z).astype(_BF16), seg)
    r = lax.rsqrt(ssq * (1.0 / HEAD_DIM) + EPS)
    return r * (z * cos_tab + _swap16(z) * sin_tab)


def _proj_kernel(x_ref, g_ref, w_ref, seg_ref, cq_ref, sq_ref, ck_ref, sk_ref,
                 u_ref, gate_ref, q_ref, qx_ref, kt_ref, v_ref):
    x = x_ref[0]
    ms = jnp.mean(x * x, axis=-1, keepdims=True)
    h = (x * lax.rsqrt(ms + EPS) * g_ref[...]).astype(_BF16)
    z = _dot(h, w_ref[...])

    u_ref[0] = z[:, _OFF_U:_OFF_U + POOL_W]
    gate_ref[0, :, 0:POOL_W] = _silu(z[:, _OFF_GP:_OFF_GP + POOL_W]).astype(_BF16)
    gate_ref[0, :, POOL_W:POOL_W + ATTN_W] = _silu(z[:, _OFF_GA:_OFF_GA + ATTN_W]).astype(_BF16)
    gate_ref[0, :, POOL_W + ATTN_W:MIX_W] = _silu(z[:, _OFF_GX:_OFF_GX + XATTN_W]).astype(_BF16)
    qx_ref[0] = (z[:, _OFF_QX:_OFF_QX + XATTN_W] * Q_SCALE).astype(_BF16)

    seg = seg_ref[...]
    cq, sq = cq_ref[...], sq_ref[...]
    for pair in range(N_HEADS // 2):
        lo = _OFF_Q + pair * LANES
        q = _norm_rope(z[:, lo:lo + LANES], seg, cq, sq)
        q_ref[0, :, pair * LANES:(pair + 1) * LANES] = q.astype(_BF16)

    k = _norm_rope(z[:, _OFF_K:_OFF_K + KV_W], seg, ck_ref[...], sk_ref[...])
    k_t = k.T
    k_t_sw = pltpu.roll(k_t, HEAD_DIM, 0)
    top = lax.broadcasted_iota(jnp.int32, k_t.shape, 0) < HEAD_DIM
    kt_ref[0, 0] = jnp.where(top, k_t, 0.0).astype(_BF16)
    kt_ref[0, 1] = jnp.where(top, 0.0, k_t_sw).astype(_BF16)
    kt_ref[0, 2] = jnp.where(top, k_t_sw, 0.0).astype(_BF16)
    kt_ref[0, 3] = jnp.where(top, 0.0, k_t).astype(_BF16)

    v = z[:, _OFF_V:_OFF_V + KV_W]
    v_sw = pltpu.roll(v, HEAD_DIM, 1)
    left = _lane_is_even_head(v.shape)
    v_ref[0, 0] = jnp.where(left, v, 0.0).astype(_BF16)
    v_ref[0, 1] = jnp.where(left, 0.0, v_sw).astype(_BF16)
    v_ref[0, 2] = jnp.where(left, v_sw, 0.0).astype(_BF16)
    v_ref[0, 3] = jnp.where(left, 0.0, v).astype(_BF16)


def _proj(x, g, w, seg, cq, sq, ck, sk):
    b, t, _ = x.shape
    rows = PROJ_ROWS
    n = t // rows
    tab = pl.BlockSpec((rows, LANES), lambda i, j: (j, 0))
    return pl.pallas_call(
        _proj_kernel,
        grid=(b, n),
        in_specs=[
            pl.BlockSpec((1, rows, D_MODEL), lambda i, j: (i, j, 0)),
            pl.BlockSpec((1, D_MODEL), lambda i, j: (0, 0)),
            pl.BlockSpec((D_MODEL, IN_W), lambda i, j: (0, 0)),
            pl.BlockSpec((LANES, LANES), lambda i, j: (0, 0)),
            tab, tab, tab, tab,
        ],
        out_specs=[
            pl.BlockSpec((1, rows, POOL_W), lambda i, j: (i, j, 0)),
            pl.BlockSpec((1, rows, MIX_W), lambda i, j: (i, j, 0)),
            pl.BlockSpec((1, rows, ATTN_W), lambda i, j: (i, j, 0)),
            pl.BlockSpec((1, rows, XATTN_W), lambda i, j: (i, j, 0)),
            pl.BlockSpec((1, 4, KV_W, rows), lambda i, j: (i, 0, 0, j)),
            pl.BlockSpec((1, 4, rows, KV_W), lambda i, j: (i, 0, j, 0)),
        ],
        out_shape=[
            jax.ShapeDtypeStruct((b, t, POOL_W), _F32),
            jax.ShapeDtypeStruct((b, t, MIX_W), _BF16),
            jax.ShapeDtypeStruct((b, t, ATTN_W), _BF16),
            jax.ShapeDtypeStruct((b, t, XATTN_W), _BF16),
            jax.ShapeDtypeStruct((b, 4, KV_W, t), _BF16),
            jax.ShapeDtypeStruct((b, 4, t, KV_W), _BF16),
        ],
        compiler_params=pltpu.CompilerParams(
            dimension_semantics=("arbitrary", "arbitrary"), vmem_limit_bytes=VMEM_LIMIT_BYTES),
        name="proj",
    )(x, g, w, seg, cq, sq, ck, sk)


def _attend(streams, kt_ref, v_ref, chunk):
    n_keys = kt_ref.shape[-1]
    left_k = _lane_is_even_head((chunk, LANES))
    ones_e = jnp.where(left_k, 1.0, 0.0).astype(_BF16)
    ones_o = jnp.where(left_k, 0.0, 1.0).astype(_BF16)
    state = [None] * len(streams)
    for c in range(n_keys // chunk):
        keys = slice(c * chunk, (c + 1) * chunk)
        for i, (q, slot) in enumerate(streams):
            s_e = _dot(q, kt_ref[slot, :, keys])
            s_o = _dot(q, kt_ref[slot + 1, :, keys])
            mn_e = jnp.max(s_e, axis=-1, keepdims=True)
            mn_o = jnp.max(s_o, axis=-1, keepdims=True)
            if state[i] is not None:
                m_e, m_o, acc = state[i]
                mn_e = jnp.maximum(m_e, mn_e)
                mn_o = jnp.maximum(m_o, mn_o)
            p_e = jnp.exp2(s_e - mn_e).astype(_BF16)
            p_o = jnp.exp2(s_o - mn_o).astype(_BF16)
            rhs_e = jnp.concatenate([v_ref[slot, keys, :], ones_e], axis=1)
            rhs_o = jnp.concatenate([v_ref[slot + 1, keys, :], ones_o], axis=1)
            pv = _dot(p_e, rhs_e) + _dot(p_o, rhs_o)
            if state[i] is not None:
                alpha = jnp.where(_lane_is_even_head(acc.shape),
                                  jnp.exp2(m_e - mn_e), jnp.exp2(m_o - mn_o))
                pv = acc * alpha + pv
            state[i] = (mn_e, mn_o, pv)
    return [acc[:, :LANES] / acc[:, LANES:] for _, _, acc in state]


def _pool_delta(u_ref, up_ref, un_ref, seq_len):
    tq = u_ref.shape[1]
    n_ext = tq + 2 * POOL_HALO
    u = u_ref[0]
    ext = jnp.concatenate([up_ref[0], u, un_ref[0]], axis=0)
    t0 = pl.program_id(1) * tq
    t_ext = t0 - POOL_HALO + lax.broadcasted_iota(jnp.int32, (n_ext, 1), 0)
    ext = jnp.where((t_ext >= 0) & (t_ext < seq_len), ext, 0.0)
    sums = {}
    run = ext
    for w in POOL_WINDOWS:
        run = run + pltpu.roll(run, n_ext - w // 2, 0)
        if w // 2 == POOL_HALO:
            sums[w] = run[0:tq]
        else:
            sums[w] = pltpu.roll(run, w // 2, 0)[POOL_HALO:POOL_HALO + tq]
    t = t0 + lax.broadcasted_iota(jnp.int32, (tq, 1), 0)
    inv = {}
    for w in POOL_WINDOWS:
        cnt = jnp.minimum(t + (w - w // 2), seq_len) - jnp.maximum(t - w // 2, 0)
        inv[w] = 1.0 / cnt.astype(_F32)
    left = _lane_is_even_head((tq, LANES))
    w0, w1, w2, w3 = POOL_WINDOWS
    mean_lo = jnp.where(left, sums[w0][:, :LANES] * inv[w0], sums[w1][:, :LANES] * inv[w1])
    mean_hi = jnp.where(left, sums[w2][:, LANES:] * inv[w2], sums[w3][:, LANES:] * inv[w3])
    return jnp.concatenate([mean_lo, mean_hi], axis=1) - u


def _mix_kernel(q_ref, qx_ref, kt_ref, v_ref, kmt_ref, vm_ref, gate_ref, u_ref, up_ref, un_ref,
                x_ref, wpool_ref, pscale_ref, wout_ref, gpost_ref, y_ref, mix_ref, *, seq_len):
    d = _pool_delta(u_ref, up_ref, un_ref, seq_len).astype(_BF16)
    pool = _dot(d, wpool_ref[...]) * pscale_ref[...]
    mix_ref[:, 0:POOL_W] = (pool * gate_ref[0, :, 0:POOL_W].astype(_F32)).astype(_BF16)

    tq = q_ref.shape[1]

    def put(out, c0):
        mix_ref[:, c0:c0 + LANES] = (out * gate_ref[0, :, c0:c0 + LANES].astype(_F32)).astype(_BF16)

    pairs_per_group = N_HEADS // N_KV_HEADS // 2
    streams = []
    for group in range(N_KV_HEADS):
        qs = [q_ref[0, :, (group * pairs_per_group + p) * LANES:(group * pairs_per_group + p + 1) * LANES]
              for p in range(pairs_per_group)]
        streams.append((jnp.concatenate(qs, axis=0), 2 * group))
    outs = _attend(streams, kt_ref.at[0], v_ref.at[0], KEY_CHUNK)
    for group in range(N_KV_HEADS):
        for p in range(pairs_per_group):
            put(outs[group][p * tq:(p + 1) * tq], POOL_W + (group * pairs_per_group + p) * LANES)

    streams = [(qx_ref[0, :, pair * LANES:(pair + 1) * LANES], 2 * pair) for pair in range(N_XHEADS // 2)]
    outs = _attend(streams, kmt_ref.at[0], vm_ref.at[0], N_MEM)
    for pair in range(N_XHEADS // 2):
        put(outs[pair], POOL_W + ATTN_W + pair * LANES)

    y = _dot(mix_ref[...], wout_ref[...])
    ms = jnp.mean(y * y, axis=-1, keepdims=True)
    y_ref[0] = x_ref[0] + y * lax.rsqrt(ms + EPS) * gpost_ref[...]


def _mix(q, qx, kt, v, kmt, vm, gate, u, x, wpool, pscale, wout, gpost):
    b, t, _ = x.shape
    tq = Q_ROWS
    n = t // tq
    halo_per_q = tq // POOL_HALO
    n_halo = t // POOL_HALO
    return pl.pallas_call(
        functools.partial(_mix_kernel, seq_len=t),
        grid=(b, n),
        in_specs=[
            pl.BlockSpec((1, tq, ATTN_W), lambda i, j: (i, j, 0)),
            pl.BlockSpec((1, tq, XATTN_W), lambda i, j: (i, j, 0)),
            pl.BlockSpec((1, 4, KV_W, t), lambda i, j: (i, 0, 0, 0)),
            pl.BlockSpec((1, 4, t, KV_W), lambda i, j: (i, 0, 0, 0)),
            pl.BlockSpec((1, N_XHEADS, LANES, N_MEM), lambda i, j: (i, 0, 0, 0)),
            pl.BlockSpec((1, N_XHEADS, N_MEM, LANES), lambda i, j: (i, 0, 0, 0)),
            pl.BlockSpec((1, tq, MIX_W), lambda i, j: (i, j, 0)),
            pl.BlockSpec((1, tq, POOL_W), lambda i, j: (i, j, 0)),
            pl.BlockSpec((1, POOL_HALO, POOL_W),
                         lambda i, j: (i, jnp.maximum(j * halo_per_q - 1, 0), 0)),
            pl.BlockSpec((1, POOL_HALO, POOL_W),
                         lambda i, j: (i, jnp.minimum((j + 1) * halo_per_q, n_halo - 1), 0)),
            pl.BlockSpec((1, tq, D_MODEL), lambda i, j: (i, j, 0)),
            pl.BlockSpec((POOL_W, POOL_W), lambda i, j: (0, 0)),
            pl.BlockSpec((1, POOL_W), lambda i, j: (0, 0)),
            pl.BlockSpec((MIX_W, D_MODEL), lambda i, j: (0, 0)),
            pl.BlockSpec((1, D_MODEL), lambda i, j: (0, 0)),
        ],
        out_specs=pl.BlockSpec((1, tq, D_MODEL), lambda i, j: (i, j, 0)),
        out_shape=jax.ShapeDtypeStruct((b, t, D_MODEL), _F32),
        scratch_shapes=[pltpu.VMEM((tq, MIX_W), _BF16)],
        compiler_params=pltpu.CompilerParams(
            dimension_semantics=("arbitrary", "arbitrary"), vmem_limit_bytes=VMEM_LIMIT_BYTES),
        name="mix",
    )(q, qx, kt, v, kmt, vm, gate, u, u, u, x, wpool, pscale, wout, gpost)


def _rope_tables(t, gain, scale):
    rows = t // GRID_W
    row = jnp.repeat(jnp.arange(rows), GRID_W).astype(_F32)
    col = jnp.tile(jnp.arange(GRID_W), rows).astype(_F32)
    freqs = ROPE_THETA ** (-jnp.arange(ROPE_PAIRS, dtype=_F32) / ROPE_PAIRS)
    ang = jnp.stack([row[:, None] * freqs, col[:, None] * freqs], axis=1)
    cos, sin = jnp.cos(ang), jnp.sin(ang)
    cos_h = jnp.stack([cos, cos], axis=2).reshape(t, HEAD_DIM)
    sin_h = jnp.stack([-sin, sin], axis=2).reshape(t, HEAD_DIM)
    g = gain.astype(_F32)
    g_sw = g.reshape(2, 2, ROPE_PAIRS)[:, ::-1, :].reshape(HEAD_DIM)
    cos_h = cos_h * (g * scale)
    sin_h = sin_h * (g_sw * scale)
    return jnp.tile(cos_h, (1, LANES // HEAD_DIM)), jnp.tile(sin_h, (1, LANES // HEAD_DIM))


def _block_diag(blocks):
    n = len(blocks)
    rows = []
    for i, blk in enumerate(blocks):
        rows.append(jnp.concatenate(
            [blk if j == i else jnp.zeros_like(blk) for j in range(n)], axis=1))
    return jnp.concatenate(rows, axis=0)


def _trunk(x, mem, norm_pre, norm_post, w_in, pool_w, pool_scale, q_norm, k_norm, mem_norm,
           w_mem_kv, w_out):
    t = x.shape[1]
    ones = jnp.ones((HEAD_DIM, HEAD_DIM), _BF16)
    seg = _block_diag([ones, ones])
    for l in range(DEPTH):
        cq, sq = _rope_tables(t, q_norm[l], Q_SCALE)
        ck, sk = _rope_tables(t, k_norm[l], 1.0)
        kmt, vm = _mem_kv(mem, mem_norm[l][None], w_mem_kv[l].astype(_BF16))
        u, gate, q, qx, kt, v = _proj(x, norm_pre[l][None], w_in[l].astype(_BF16), seg,
                                      cq, sq, ck, sk)
        wpool = _block_diag([pool_w[l, g] for g in range(len(POOL_WINDOWS))]).astype(_BF16)
        x = _mix(q, qx, kt, v, kmt, vm, gate, u, x, wpool, pool_scale[l][None],
                 w_out[l].astype(_BF16), norm_post[l][None])
    return x


def kernel(x_prompt, x_sample, mem_prompt, mem_sample, norm_pre, norm_post, w_in, pool_w, pool_scale,
           q_norm, k_norm, mem_norm, w_mem_kv, w_out):
    weights = (norm_pre, norm_post, w_in, pool_w, pool_scale, q_norm, k_norm, mem_norm, w_mem_kv, w_out)
    return (_trunk(x_prompt, mem_prompt, *weights), _trunk(x_sample, mem_sample, *weights))
```

```python
import functools
import math

import jax
import jax.numpy as jnp
from jax import lax
from jax.experimental import pallas as pl
from jax.experimental.pallas import tpu as pltpu

D_MODEL = 1024
DEPTH = 2
GRID_W = 64
N_MEM = 256
HEAD_DIM = 64
N_HEADS = 8
N_KV_HEADS = 2
ATTN_W = N_HEADS * HEAD_DIM
KV_W = N_KV_HEADS * HEAD_DIM
POOL_W = 256
POOL_WINDOWS = (2, 4, 8, 16)
POOL_GROUP = POOL_W // len(POOL_WINDOWS)
N_XHEADS = 4
XATTN_W = N_XHEADS * HEAD_DIM
MIX_W = POOL_W + ATTN_W + XATTN_W
IN_W = 2 * POOL_W + 2 * ATTN_W + 2 * KV_W + 2 * XATTN_W
ROPE_THETA = 10000.0
ROPE_PAIRS = HEAD_DIM // 4
EPS = 1e-6
Q_SCALE = math.log2(math.e) * HEAD_DIM ** -0.5

LANES = 128
BF16_SUBLANES = 16
POOL_HALO = 8
PROJ_ROWS = 512
KEY_CHUNK = 256
Q_ROWS = 256
ATTN_LOOKAHEAD = 2
VMEM_LIMIT_BYTES = 56 * 1024 * 1024

VT_ROWS = LANES + BF16_SUBLANES

_OFF_U = 0
_OFF_GP = POOL_W
_OFF_Q = 2 * POOL_W
_OFF_K = _OFF_Q + ATTN_W
_OFF_V = _OFF_K + KV_W
_OFF_GA = _OFF_V + KV_W
_OFF_QX = _OFF_GA + ATTN_W
_OFF_GX = _OFF_QX + XATTN_W

_BF16 = jnp.bfloat16
_F32 = jnp.float32


def _dot(a, b):
    return jnp.dot(a, b, preferred_element_type=_F32)


def _silu(g):
    return g * (1.0 / (1.0 + jnp.exp(-g)))


def _is_even_head_lane(shape):
    return (lax.broadcasted_iota(jnp.int32, shape, len(shape) - 1) % LANES) < HEAD_DIM


def _is_even_head_row(shape):
    return lax.broadcasted_iota(jnp.int32, shape, 0) < HEAD_DIM


def _ones_rows(parity, cols):
    row = lax.broadcasted_iota(jnp.int32, (BF16_SUBLANES, cols), 0)
    return jnp.where(row == parity, 1.0, 0.0).astype(_BF16)


def _mem_kv_kernel(mem_ref, g_ref, w_ref, k_ref, vt_ref):
    m = mem_ref[0]
    ms = jnp.mean(m * m, axis=-1, keepdims=True)
    mh = (m * lax.rsqrt(ms + EPS) * g_ref[...]).astype(_BF16)
    kv = _dot(mh, w_ref[...])
    k = kv[:, :XATTN_W]
    v_t = kv[:, XATTN_W:].T
    left = _is_even_head_lane((N_MEM, LANES))
    top = _is_even_head_row((LANES, N_MEM))
    for pair in range(N_XHEADS // 2):
        kp = k[:, pair * LANES:(pair + 1) * LANES]
        vp = v_t[pair * LANES:(pair + 1) * LANES]
        k_ref[0, 2 * pair] = jnp.where(left, kp, 0.0).astype(_BF16)
        k_ref[0, 2 * pair + 1] = jnp.where(left, 0.0, kp).astype(_BF16)
        vt_ref[0, 2 * pair, 0:LANES] = jnp.where(top, vp, 0.0).astype(_BF16)
        vt_ref[0, 2 * pair + 1, 0:LANES] = jnp.where(top, 0.0, vp).astype(_BF16)
        for parity in range(2):
            vt_ref[0, 2 * pair + parity, LANES:VT_ROWS] = _ones_rows(parity, N_MEM)


def _mem_kv(mem, g, w):
    b = mem.shape[0]
    return pl.pallas_call(
        _mem_kv_kernel,
        grid=(b,),
        in_specs=[
            pl.BlockSpec((1, N_MEM, D_MODEL), lambda i: (i, 0, 0)),
            pl.BlockSpec((1, D_MODEL), lambda i: (0, 0)),
            pl.BlockSpec((D_MODEL, 2 * XATTN_W), lambda i: (0, 0)),
        ],
        out_specs=[
            pl.BlockSpec((1, N_XHEADS, N_MEM, LANES), lambda i: (i, 0, 0, 0)),
            pl.BlockSpec((1, N_XHEADS, VT_ROWS, N_MEM), lambda i: (i, 0, 0, 0)),
        ],
        out_shape=[
            jax.ShapeDtypeStruct((b, N_XHEADS, N_MEM, LANES), _BF16),
            jax.ShapeDtypeStruct((b, N_XHEADS, VT_ROWS, N_MEM), _BF16),
        ],
        compiler_params=pltpu.CompilerParams(
            dimension_semantics=("arbitrary",), vmem_limit_bytes=VMEM_LIMIT_BYTES),
        name="mem_kv",
    )(mem, g, w)


def _swap16(x):
    n = x.shape[-1]
    lane = lax.broadcasted_iota(jnp.int32, x.shape, 1)
    first = (lane & ROPE_PAIRS) == 0
    return jnp.where(first, pltpu.roll(x, n - ROPE_PAIRS, 1), pltpu.roll(x, ROPE_PAIRS, 1))


def _norm_rope(z, seg, cos_tab, sin_tab):
    ssq = _dot((z * z).astype(_BF16), seg)
    r = lax.rsqrt(ssq * (1.0 / HEAD_DIM) + EPS)
    return r * (z * cos_tab + _swap16(z) * sin_tab)


def _proj_kernel(x_ref, g_ref, w_ref, seg_ref, cq_ref, sq_ref, ck_ref, sk_ref,
                 u_ref, gate_ref, qt_ref, qxt_ref, k_ref, vt_ref):
    x = x_ref[0]
    rows = x.shape[0]
    ms = jnp.mean(x * x, axis=-1, keepdims=True)
    h = (x * lax.rsqrt(ms + EPS) * g_ref[...]).astype(_BF16)
    z = _dot(h, w_ref[...])

    u_ref[0] = z[:, _OFF_U:_OFF_U + POOL_W]
    gate_ref[0, :, 0:POOL_W] = _silu(z[:, _OFF_GP:_OFF_GP + POOL_W]).astype(_BF16)
    gate_ref[0, :, POOL_W:POOL_W + ATTN_W] = _silu(z[:, _OFF_GA:_OFF_GA + ATTN_W]).astype(_BF16)
    gate_ref[0, :, POOL_W + ATTN_W:MIX_W] = _silu(z[:, _OFF_GX:_OFF_GX + XATTN_W]).astype(_BF16)
    for pair in range(N_XHEADS // 2):
        lo = _OFF_QX + pair * LANES
        qxt_ref[0, pair] = (z[:, lo:lo + LANES] * Q_SCALE).T.astype(_BF16)

    seg = seg_ref[...]
    cq, sq = cq_ref[...], sq_ref[...]
    for pair in range(N_HEADS // 2):
        lo = _OFF_Q + pair * LANES
        qt_ref[0, pair] = _norm_rope(z[:, lo:lo + LANES], seg, cq, sq).T.astype(_BF16)

    k = _norm_rope(z[:, _OFF_K:_OFF_K + KV_W], seg, ck_ref[...], sk_ref[...])
    k_sw = pltpu.roll(k, HEAD_DIM, 1)
    left = _is_even_head_lane(k.shape)
    k_ref[0, 0] = jnp.where(left, k, 0.0).astype(_BF16)
    k_ref[0, 1] = jnp.where(left, 0.0, k_sw).astype(_BF16)
    k_ref[0, 2] = jnp.where(left, k_sw, 0.0).astype(_BF16)
    k_ref[0, 3] = jnp.where(left, 0.0, k).astype(_BF16)

    v_t = z[:, _OFF_V:_OFF_V + KV_W].T
    v_t_sw = pltpu.roll(v_t, HEAD_DIM, 0)
    top = _is_even_head_row(v_t.shape)
    vt_ref[0, 0, 0:LANES] = jnp.where(top, v_t, 0.0).astype(_BF16)
    vt_ref[0, 1, 0:LANES] = jnp.where(top, 0.0, v_t_sw).astype(_BF16)
    vt_ref[0, 2, 0:LANES] = jnp.where(top, v_t_sw, 0.0).astype(_BF16)
    vt_ref[0, 3, 0:LANES] = jnp.where(top, 0.0, v_t).astype(_BF16)
    for slot in range(2 * N_KV_HEADS):
        vt_ref[0, slot, LANES:VT_ROWS] = _ones_rows(slot % 2, rows)


def _proj(x, g, w, seg, cq, sq, ck, sk):
    b, t, _ = x.shape
    rows = PROJ_ROWS
    n = t // rows
    tab = pl.BlockSpec((rows, LANES), lambda i, j: (j, 0))
    return pl.pallas_call(
        _proj_kernel,
        grid=(b, n),
        in_specs=[
            pl.BlockSpec((1, rows, D_MODEL), lambda i, j: (i, j, 0)),
            pl.BlockSpec((1, D_MODEL), lambda i, j: (0, 0)),
            pl.BlockSpec((D_MODEL, IN_W), lambda i, j: (0, 0)),
            pl.BlockSpec((LANES, LANES), lambda i, j: (0, 0)),
            tab, tab, tab, tab,
        ],
        out_specs=[
            pl.BlockSpec((1, rows, POOL_W), lambda i, j: (i, j, 0)),
            pl.BlockSpec((1, rows, MIX_W), lambda i, j: (i, j, 0)),
            pl.BlockSpec((1, N_HEADS // 2, LANES, rows), lambda i, j: (i, 0, 0, j)),
            pl.BlockSpec((1, N_XHEADS // 2, LANES, rows), lambda i, j: (i, 0, 0, j)),
            pl.BlockSpec((1, 2 * N_KV_HEADS, rows, KV_W), lambda i, j: (i, 0, j, 0)),
            pl.BlockSpec((1, 2 * N_KV_HEADS, VT_ROWS, rows), lambda i, j: (i, 0, 0, j)),
        ],
        out_shape=[
            jax.ShapeDtypeStruct((b, t, POOL_W), _F32),
            jax.ShapeDtypeStruct((b, t, MIX_W), _BF16),
            jax.ShapeDtypeStruct((b, N_HEADS // 2, LANES, t), _BF16),
            jax.ShapeDtypeStruct((b, N_XHEADS // 2, LANES, t), _BF16),
            jax.ShapeDtypeStruct((b, 2 * N_KV_HEADS, t, KV_W), _BF16),
            jax.ShapeDtypeStruct((b, 2 * N_KV_HEADS, VT_ROWS, t), _BF16),
        ],
        compiler_params=pltpu.CompilerParams(
            dimension_semantics=("arbitrary", "arbitrary"), vmem_limit_bytes=VMEM_LIMIT_BYTES),
        name="proj",
    )(x, g, w, seg, cq, sq, ck, sk)


def _attend(streams, lookahead):
    row = lax.broadcasted_iota(jnp.int32, (VT_ROWS, 1), 0)
    even_acc_row = (row < HEAD_DIM) | (row == LANES)
    n_chunks = [k_ref.shape[1] // chunk for _, k_ref, _, _, chunk in streams]
    units = [(c, i) for c in range(max(n_chunks)) for i in range(len(streams)) if c < n_chunks[i]]
    state = [None] * len(streams)
    scores = {}

    def issue_scores(c, i):
        q_t, k_ref, _, slot, chunk = streams[i]
        keys = slice(c * chunk, (c + 1) * chunk)
        return (_dot(k_ref[slot, keys, :], q_t), _dot(k_ref[slot + 1, keys, :], q_t))

    def finish(c, i, s_e, s_o):
        _, _, vt_ref, slot, chunk = streams[i]
        keys = slice(c * chunk, (c + 1) * chunk)
        mn_e = jnp.max(s_e, axis=0, keepdims=True)
        mn_o = jnp.max(s_o, axis=0, keepdims=True)
        if state[i] is not None:
            m_e, m_o, acc = state[i]
            mn_e = jnp.maximum(m_e, mn_e)
            mn_o = jnp.maximum(m_o, mn_o)
        p_e = jnp.exp2((s_e - mn_e).astype(_BF16))
        p_o = jnp.exp2((s_o - mn_o).astype(_BF16))
        pv = _dot(vt_ref[slot, :, keys], p_e) + _dot(vt_ref[slot + 1, :, keys], p_o)
        if state[i] is not None:
            alpha = jnp.where(even_acc_row, jnp.exp2(m_e - mn_e), jnp.exp2(m_o - mn_o))
            pv = acc * alpha + pv
        state[i] = (mn_e, mn_o, pv)

    for step in range(len(units) + lookahead):
        if step < len(units):
            scores[step] = issue_scores(*units[step])
        if step >= lookahead:
            finish(*units[step - lookahead], *scores.pop(step - lookahead))

    outs = []
    top = _is_even_head_row((LANES, 1))
    for _, _, acc in state:
        inv = jnp.where(top, 1.0 / acc[LANES:LANES + 1], 1.0 / acc[LANES + 1:LANES + 2])
        outs.append((acc[0:LANES] * inv).T)
    return outs


def _pool_delta(u_ref, up_ref, un_ref, seq_len):
    tq = u_ref.shape[1]
    n_ext = tq + 2 * POOL_HALO
    u = u_ref[0]
    ext = jnp.concatenate([up_ref[0], u, un_ref[0]], axis=0)
    t0 = pl.program_id(1) * tq
    t_ext = t0 - POOL_HALO + lax.broadcasted_iota(jnp.int32, (n_ext, 1), 0)
    ext = jnp.where((t_ext >= 0) & (t_ext < seq_len), ext, 0.0)
    sums = {}
    run = ext
    for w in POOL_WINDOWS:
        run = run + pltpu.roll(run, n_ext - w // 2, 0)
        if w // 2 == POOL_HALO:
            sums[w] = run[0:tq]
        else:
            sums[w] = pltpu.roll(run, w // 2, 0)[POOL_HALO:POOL_HALO + tq]
    t = t0 + lax.broadcasted_iota(jnp.int32, (tq, 1), 0)
    inv = {}
    for w in POOL_WINDOWS:
        cnt = jnp.minimum(t + (w - w // 2), seq_len) - jnp.maximum(t - w // 2, 0)
        inv[w] = 1.0 / cnt.astype(_F32)
    left = _is_even_head_lane((tq, LANES))
    w0, w1, w2, w3 = POOL_WINDOWS
    mean_lo = jnp.where(left, sums[w0][:, :LANES] * inv[w0], sums[w1][:, :LANES] * inv[w1])
    mean_hi = jnp.where(left, sums[w2][:, LANES:] * inv[w2], sums[w3][:, LANES:] * inv[w3])
    return jnp.concatenate([mean_lo, mean_hi], axis=1) - u


def _mix_kernel(qt_ref, qxt_ref, k_ref, vt_ref, km_ref, vmt_ref, gate_ref, u_ref, up_ref, un_ref,
                x_ref, wpool_ref, pscale_ref, wout_ref, gpost_ref, y_ref, mix_ref, *, seq_len):
    d = _pool_delta(u_ref, up_ref, un_ref, seq_len).astype(_BF16)
    pool = _dot(d, wpool_ref[...]) * pscale_ref[...]
    mix_ref[:, 0:POOL_W] = (pool * gate_ref[0, :, 0:POOL_W].astype(_F32)).astype(_BF16)

    tq = qt_ref.shape[3]

    def put(out, c0):
        mix_ref[:, c0:c0 + LANES] = (out * gate_ref[0, :, c0:c0 + LANES].astype(_F32)).astype(_BF16)

    pairs_per_group = N_HEADS // N_KV_HEADS // 2
    streams = [(qxt_ref[0, pair], km_ref.at[0], vmt_ref.at[0], 2 * pair, N_MEM)
               for pair in range(N_XHEADS // 2)]
    for group in range(N_KV_HEADS):
        q_t = jnp.concatenate(
            [qt_ref[0, group * pairs_per_group + p] for p in range(pairs_per_group)], axis=1)
        streams.append((q_t, k_ref.at[0], vt_ref.at[0], 2 * group, KEY_CHUNK))
    outs = _attend(streams, ATTN_LOOKAHEAD)
    for pair in range(N_XHEADS // 2):
        put(outs[pair], POOL_W + ATTN_W + pair * LANES)
    for group in range(N_KV_HEADS):
        out = outs[N_XHEADS // 2 + group]
        for p in range(pairs_per_group):
            put(out[p * tq:(p + 1) * tq], POOL_W + (group * pairs_per_group + p) * LANES)

    y = _dot(mix_ref[...], wout_ref[...])
    ms = jnp.mean(y * y, axis=-1, keepdims=True)
    y_ref[0] = x_ref[0] + y * lax.rsqrt(ms + EPS) * gpost_ref[...]


def _mix(qt, qxt, k, vt, km, vmt, gate, u, x, wpool, pscale, wout, gpost):
    b, t, _ = x.shape
    tq = Q_ROWS
    n = t // tq
    halo_per_q = tq // POOL_HALO
    n_halo = t // POOL_HALO
    return pl.pallas_call(
        functools.partial(_mix_kernel, seq_len=t),
        grid=(b, n),
        in_specs=[
            pl.BlockSpec((1, N_HEADS // 2, LANES, tq), lambda i, j: (i, 0, 0, j)),
            pl.BlockSpec((1, N_XHEADS // 2, LANES, tq), lambda i, j: (i, 0, 0, j)),
            pl.BlockSpec((1, 2 * N_KV_HEADS, t, KV_W), lambda i, j: (i, 0, 0, 0)),
            pl.BlockSpec((1, 2 * N_KV_HEADS, VT_ROWS, t), lambda i, j: (i, 0, 0, 0)),
            pl.BlockSpec((1, N_XHEADS, N_MEM, LANES), lambda i, j: (i, 0, 0, 0)),
            pl.BlockSpec((1, N_XHEADS, VT_ROWS, N_MEM), lambda i, j: (i, 0, 0, 0)),
            pl.BlockSpec((1, tq, MIX_W), lambda i, j: (i, j, 0)),
            pl.BlockSpec((1, tq, POOL_W), lambda i, j: (i, j, 0)),
            pl.BlockSpec((1, POOL_HALO, POOL_W),
                         lambda i, j: (i, jnp.maximum(j * halo_per_q - 1, 0), 0)),
            pl.BlockSpec((1, POOL_HALO, POOL_W),
                         lambda i, j: (i, jnp.minimum((j + 1) * halo_per_q, n_halo - 1), 0)),
            pl.BlockSpec((1, tq, D_MODEL), lambda i, j: (i, j, 0)),
            pl.BlockSpec((POOL_W, POOL_W), lambda i, j: (0, 0)),
            pl.BlockSpec((1, POOL_W), lambda i, j: (0, 0)),
            pl.BlockSpec((MIX_W, D_MODEL), lambda i, j: (0, 0)),
            pl.BlockSpec((1, D_MODEL), lambda i, j: (0, 0)),
        ],
        out_specs=pl.BlockSpec((1, tq, D_MODEL), lambda i, j: (i, j, 0)),
        out_shape=jax.ShapeDtypeStruct((b, t, D_MODEL), _F32),
        scratch_shapes=[pltpu.VMEM((tq, MIX_W), _BF16)],
        compiler_params=pltpu.CompilerParams(
            dimension_semantics=("arbitrary", "arbitrary"), vmem_limit_bytes=VMEM_LIMIT_BYTES),
        name="mix",
    )(qt, qxt, k, vt, km, vmt, gate, u, u, u, x, wpool, pscale, wout, gpost)


def _rope_tables(t, gain, scale):
    rows = t // GRID_W
    row = jnp.repeat(jnp.arange(rows), GRID_W).astype(_F32)
    col = jnp.tile(jnp.arange(GRID_W), rows).astype(_F32)
    freqs = ROPE_THETA ** (-jnp.arange(ROPE_PAIRS, dtype=_F32) / ROPE_PAIRS)
    ang = jnp.stack([row[:, None] * freqs, col[:, None] * freqs], axis=1)
    cos, sin = jnp.cos(ang), jnp.sin(ang)
    cos_h = jnp.stack([cos, cos], axis=2).reshape(t, HEAD_DIM)
    sin_h = jnp.stack([-sin, sin], axis=2).reshape(t, HEAD_DIM)
    g = gain.astype(_F32)
    g_sw = g.reshape(2, 2, ROPE_PAIRS)[:, ::-1, :].reshape(HEAD_DIM)
    cos_h = cos_h * (g * scale)
    sin_h = sin_h * (g_sw * scale)
    return jnp.tile(cos_h, (1, LANES // HEAD_DIM)), jnp.tile(sin_h, (1, LANES // HEAD_DIM))


def _block_diag(blocks):
    n = len(blocks)
    rows = []
    for i, blk in enumerate(blocks):
        rows.append(jnp.concatenate(
            [blk if j == i else jnp.zeros_like(blk) for j in range(n)], axis=1))
    return jnp.concatenate(rows, axis=0)


def _trunk(x, mem, norm_pre, norm_post, w_in, pool_w, pool_scale, q_norm, k_norm, mem_norm,
           w_mem_kv, w_out):
    t = x.shape[1]
    ones = jnp.ones((HEAD_DIM, HEAD_DIM), _BF16)
    seg = _block_diag([ones, ones])
    for l in range(DEPTH):
        cq, sq = _rope_tables(t, q_norm[l], Q_SCALE)
        ck, sk = _rope_tables(t, k_norm[l], 1.0)
        km, vmt = _mem_kv(mem, mem_norm[l][None], w_mem_kv[l].astype(_BF16))
        u, gate, qt, qxt, k, vt = _proj(x, norm_pre[l][None], w_in[l].astype(_BF16), seg,
                                        cq, sq, ck, sk)
        wpool = _block_diag([pool_w[l, g] for g in range(len(POOL_WINDOWS))]).astype(_BF16)
        x = _mix(qt, qxt, k, vt, km, vmt, gate, u, x, wpool, pool_scale[l][None],
                 w_out[l].astype(_BF16), norm_post[l][None])
    return x


def kernel(x_prompt, x_sample, mem_prompt, mem_sample, norm_pre, norm_post, w_in, pool_w, pool_scale,
           q_norm, k_norm, mem_norm, w_mem_kv, w_out):
    weights = (norm_pre, norm_post, w_in, pool_w, pool_scale, q_norm, k_norm, mem_norm, w_mem_kv, w_out)
    return (_trunk(x_prompt, mem_prompt, *weights), _trunk(x_sample, mem_sample, *weights))
```

```python
import functools
import math

import jax
import jax.numpy as jnp
from jax import lax
from jax.experimental import pallas as pl
from jax.experimental.pallas import tpu as pltpu

D_MODEL = 1024
DEPTH = 2
GRID_W = 64
N_MEM = 256
HEAD_DIM = 64
N_HEADS = 8
N_KV_HEADS = 2
ATTN_W = N_HEADS * HEAD_DIM
KV_W = N_KV_HEADS * HEAD_DIM
POOL_W = 256
POOL_WINDOWS = (2, 4, 8, 16)
POOL_GROUP = POOL_W // len(POOL_WINDOWS)
N_XHEADS = 4
XATTN_W = N_XHEADS * HEAD_DIM
MIX_W = POOL_W + ATTN_W + XATTN_W
IN_W = 2 * POOL_W + 2 * ATTN_W + 2 * KV_W + 2 * XATTN_W
ROPE_THETA = 10000.0
ROPE_PAIRS = HEAD_DIM // 4
EPS = 1e-6
Q_SCALE = math.log2(math.e) * HEAD_DIM ** -0.5

LANES = 128
BF16_SUBLANES = 16
POOL_HALO = 8
PROJ_ROWS = 512
KEY_CHUNK = 256
Q_ROWS = 256
ATTN_LOOKAHEAD = 2
VMEM_LIMIT_BYTES = 56 * 1024 * 1024

VT_ROWS = HEAD_DIM + BF16_SUBLANES

_OFF_U = 0
_OFF_GP = POOL_W
_OFF_Q = 2 * POOL_W
_OFF_K = _OFF_Q + ATTN_W
_OFF_V = _OFF_K + KV_W
_OFF_GA = _OFF_V + KV_W
_OFF_QX = _OFF_GA + ATTN_W
_OFF_GX = _OFF_QX + XATTN_W

_BF16 = jnp.bfloat16
_F32 = jnp.float32


def _dot(a, b):
    return jnp.dot(a, b, preferred_element_type=_F32)


def _silu(g):
    return g * (1.0 / (1.0 + jnp.exp(-g)))


def _is_even_head_lane(shape):
    return (lax.broadcasted_iota(jnp.int32, shape, len(shape) - 1) % LANES) < HEAD_DIM


def _ones_row_tile(cols):
    row = lax.broadcasted_iota(jnp.int32, (BF16_SUBLANES, cols), 0)
    return jnp.where(row == 0, 1.0, 0.0).astype(_BF16)


def _mem_kv_kernel(mem_ref, g_ref, w_ref, k_ref, vt_ref):
    m = mem_ref[0]
    ms = jnp.mean(m * m, axis=-1, keepdims=True)
    mh = (m * lax.rsqrt(ms + EPS) * g_ref[...]).astype(_BF16)
    kv = _dot(mh, w_ref[...])
    k = kv[:, :XATTN_W]
    v_t = kv[:, XATTN_W:].T
    left = _is_even_head_lane((N_MEM, LANES))
    for pair in range(N_XHEADS // 2):
        kp = k[:, pair * LANES:(pair + 1) * LANES]
        k_ref[0, 2 * pair] = jnp.where(left, kp, 0.0).astype(_BF16)
        k_ref[0, 2 * pair + 1] = jnp.where(left, 0.0, kp).astype(_BF16)
    for head in range(N_XHEADS):
        vt_ref[0, head, 0:HEAD_DIM] = v_t[head * HEAD_DIM:(head + 1) * HEAD_DIM].astype(_BF16)
        vt_ref[0, head, HEAD_DIM:VT_ROWS] = _ones_row_tile(N_MEM)


def _mem_kv(mem, g, w):
    b = mem.shape[0]
    return pl.pallas_call(
        _mem_kv_kernel,
        grid=(b,),
        in_specs=[
            pl.BlockSpec((1, N_MEM, D_MODEL), lambda i: (i, 0, 0)),
            pl.BlockSpec((1, D_MODEL), lambda i: (0, 0)),
            pl.BlockSpec((D_MODEL, 2 * XATTN_W), lambda i: (0, 0)),
        ],
        out_specs=[
            pl.BlockSpec((1, N_XHEADS, N_MEM, LANES), lambda i: (i, 0, 0, 0)),
            pl.BlockSpec((1, N_XHEADS, VT_ROWS, N_MEM), lambda i: (i, 0, 0, 0)),
        ],
        out_shape=[
            jax.ShapeDtypeStruct((b, N_XHEADS, N_MEM, LANES), _BF16),
            jax.ShapeDtypeStruct((b, N_XHEADS, VT_ROWS, N_MEM), _BF16),
        ],
        compiler_params=pltpu.CompilerParams(
            dimension_semantics=("arbitrary",), vmem_limit_bytes=VMEM_LIMIT_BYTES),
        name="mem_kv",
    )(mem, g, w)


def _swap16(x):
    n = x.shape[-1]
    lane = lax.broadcasted_iota(jnp.int32, x.shape, 1)
    first = (lane & ROPE_PAIRS) == 0
    return jnp.where(first, pltpu.roll(x, n - ROPE_PAIRS, 1), pltpu.roll(x, ROPE_PAIRS, 1))


def _norm_rope(z, seg, cos_tab, sin_tab):
    ssq = _dot((z * z).astype(_BF16), seg)
    r = lax.rsqrt(ssq * (1.0 / HEAD_DIM) + EPS)
    return r * (z * cos_tab + _swap16(z) * sin_tab)


def _proj_kernel(x_ref, g_ref, w_ref, seg_ref, cq_ref, sq_ref, ck_ref, sk_ref,
                 u_ref, gate_ref, qt_ref, qxt_ref, k_ref, vt_ref):
    x = x_ref[0]
    rows = x.shape[0]
    ms = jnp.mean(x * x, axis=-1, keepdims=True)
    h = (x * lax.rsqrt(ms + EPS) * g_ref[...]).astype(_BF16)
    z = _dot(h, w_ref[...])

    u_ref[0] = z[:, _OFF_U:_OFF_U + POOL_W]
    gate_ref[0, :, 0:POOL_W] = _silu(z[:, _OFF_GP:_OFF_GP + POOL_W]).astype(_BF16)
    gate_ref[0, :, POOL_W:POOL_W + ATTN_W] = _silu(z[:, _OFF_GA:_OFF_GA + ATTN_W]).astype(_BF16)
    gate_ref[0, :, POOL_W + ATTN_W:MIX_W] = _silu(z[:, _OFF_GX:_OFF_GX + XATTN_W]).astype(_BF16)
    for pair in range(N_XHEADS // 2):
        lo = _OFF_QX + pair * LANES
        qxt_ref[0, pair] = (z[:, lo:lo + LANES] * Q_SCALE).T.astype(_BF16)

    seg = seg_ref[...]
    cq, sq = cq_ref[...], sq_ref[...]
    for pair in range(N_HEADS // 2):
        lo = _OFF_Q + pair * LANES
        qt_ref[0, pair] = _norm_rope(z[:, lo:lo + LANES], seg, cq, sq).T.astype(_BF16)

    k = _norm_rope(z[:, _OFF_K:_OFF_K + KV_W], seg, ck_ref[...], sk_ref[...])
    k_sw = pltpu.roll(k, HEAD_DIM, 1)
    left = _is_even_head_lane(k.shape)
    k_ref[0, 0] = jnp.where(left, k, 0.0).astype(_BF16)
    k_ref[0, 1] = jnp.where(left, 0.0, k_sw).astype(_BF16)
    k_ref[0, 2] = jnp.where(left, k_sw, 0.0).astype(_BF16)
    k_ref[0, 3] = jnp.where(left, 0.0, k).astype(_BF16)

    v_t = z[:, _OFF_V:_OFF_V + KV_W].T
    for head in range(N_KV_HEADS):
        vt_ref[0, head, 0:HEAD_DIM] = v_t[head * HEAD_DIM:(head + 1) * HEAD_DIM].astype(_BF16)
        vt_ref[0, head, HEAD_DIM:VT_ROWS] = _ones_row_tile(rows)


def _proj(x, g, w, seg, cq, sq, ck, sk):
    b, t, _ = x.shape
    rows = PROJ_ROWS
    n = t // rows
    tab = pl.BlockSpec((rows, LANES), lambda i, j: (j, 0))
    return pl.pallas_call(
        _proj_kernel,
        grid=(b, n),
        in_specs=[
            pl.BlockSpec((1, rows, D_MODEL), lambda i, j: (i, j, 0)),
            pl.BlockSpec((1, D_MODEL), lambda i, j: (0, 0)),
            pl.BlockSpec((D_MODEL, IN_W), lambda i, j: (0, 0)),
            pl.BlockSpec((LANES, LANES), lambda i, j: (0, 0)),
            tab, tab, tab, tab,
        ],
        out_specs=[
            pl.BlockSpec((1, rows, POOL_W), lambda i, j: (i, j, 0)),
            pl.BlockSpec((1, rows, MIX_W), lambda i, j: (i, j, 0)),
            pl.BlockSpec((1, N_HEADS // 2, LANES, rows), lambda i, j: (i, 0, 0, j)),
            pl.BlockSpec((1, N_XHEADS // 2, LANES, rows), lambda i, j: (i, 0, 0, j)),
            pl.BlockSpec((1, 2 * N_KV_HEADS, rows, KV_W), lambda i, j: (i, 0, j, 0)),
            pl.BlockSpec((1, N_KV_HEADS, VT_ROWS, rows), lambda i, j: (i, 0, 0, j)),
        ],
        out_shape=[
            jax.ShapeDtypeStruct((b, t, POOL_W), _F32),
            jax.ShapeDtypeStruct((b, t, MIX_W), _BF16),
            jax.ShapeDtypeStruct((b, N_HEADS // 2, LANES, t), _BF16),
            jax.ShapeDtypeStruct((b, N_XHEADS // 2, LANES, t), _BF16),
            jax.ShapeDtypeStruct((b, 2 * N_KV_HEADS, t, KV_W), _BF16),
            jax.ShapeDtypeStruct((b, N_KV_HEADS, VT_ROWS, t), _BF16),
        ],
        compiler_params=pltpu.CompilerParams(
            dimension_semantics=("arbitrary", "arbitrary"), vmem_limit_bytes=VMEM_LIMIT_BYTES),
        name="proj",
    )(x, g, w, seg, cq, sq, ck, sk)


def _attend(streams, lookahead):
    n_chunks = [s[1].shape[1] // s[5] for s in streams]
    units = [(c, i) for c in range(max(n_chunks)) for i in range(len(streams)) if c < n_chunks[i]]
    state = [None] * len(streams)
    scores = {}

    def issue_scores(c, i):
        q_t, k_ref, k_slot, _, _, chunk = streams[i]
        keys = slice(c * chunk, (c + 1) * chunk)
        return [_dot(k_ref[k_slot + parity, keys, :], q_t) for parity in range(2)]

    def finish(c, i, s):
        _, _, _, vt_ref, v_slots, chunk = streams[i]
        keys = slice(c * chunk, (c + 1) * chunk)
        new = []
        for parity in range(2):
            mn = jnp.max(s[parity], axis=0, keepdims=True)
            if state[i] is not None:
                m, acc = state[i][parity]
                mn = jnp.maximum(m, mn)
            p = jnp.exp2((s[parity] - mn).astype(_BF16))
            pv = _dot(vt_ref[v_slots[parity], :, keys], p)
            if state[i] is not None:
                pv = acc * jnp.exp2(m - mn) + pv
            new.append((mn, pv))
        state[i] = new

    for step in range(len(units) + lookahead):
        if step < len(units):
            scores[step] = issue_scores(*units[step])
        if step >= lookahead:
            finish(*units[step - lookahead], scores.pop(step - lookahead))

    outs = []
    for halves in state:
        out_t = [acc[0:HEAD_DIM] * (1.0 / acc[HEAD_DIM:HEAD_DIM + 1]) for _, acc in halves]
        outs.append(jnp.concatenate(out_t, axis=0).T)
    return outs


def _pool_delta(u_ref, up_ref, un_ref, seq_len):
    tq = u_ref.shape[1]
    n_ext = tq + 2 * POOL_HALO
    t0 = pl.program_id(1) * tq
    t_ext = t0 - POOL_HALO + lax.broadcasted_iota(jnp.int32, (n_ext, LANES), 0)
    in_seq = (t_ext >= 0) & (t_ext < seq_len)
    t = t0 + lax.broadcasted_iota(jnp.int32, (tq, LANES), 0)
    left = _is_even_head_lane((tq, LANES))
    deltas = []
    for half in range(POOL_W // LANES):
        w_a, w_b = POOL_WINDOWS[2 * half], POOL_WINDOWS[2 * half + 1]
        cols = slice(half * LANES, (half + 1) * LANES)
        u = u_ref[0, :, cols]
        ext = jnp.concatenate([up_ref[0, :, cols], u, un_ref[0, :, cols]], axis=0)
        ext = jnp.where(in_seq, ext, 0.0)
        sums = {}
        run = ext
        for w in POOL_WINDOWS[:2 * half + 2]:
            run = run + pltpu.roll(run, n_ext - w // 2, 0)
            if w in (w_a, w_b):
                aligned = run if w // 2 == POOL_HALO else pltpu.roll(run, w // 2, 0)
                first = 0 if w // 2 == POOL_HALO else POOL_HALO
                sums[w] = aligned[first:first + tq]
        reach = jnp.where(left, w_a // 2, w_b // 2)
        cnt = jnp.minimum(t + reach, seq_len) - jnp.maximum(t - reach, 0)
        mean = jnp.where(left, sums[w_a], sums[w_b]) * (1.0 / cnt.astype(_F32))
        deltas.append(mean - u)
    return jnp.concatenate(deltas, axis=1)


def _mix_kernel(qt_ref, qxt_ref, k_ref, vt_ref, km_ref, vmt_ref, gate_ref, u_ref, up_ref, un_ref,
                x_ref, wpool_ref, pscale_ref, wout_ref, gpost_ref, y_ref, mix_ref, *, seq_len):
    d = _pool_delta(u_ref, up_ref, un_ref, seq_len).astype(_BF16)
    pool = _dot(d, wpool_ref[...]) * pscale_ref[...]
    mix_ref[:, 0:POOL_W] = (pool * gate_ref[0, :, 0:POOL_W].astype(_F32)).astype(_BF16)

    tq = qt_ref.shape[3]

    def put(out, c0):
        mix_ref[:, c0:c0 + LANES] = (out * gate_ref[0, :, c0:c0 + LANES].astype(_F32)).astype(_BF16)

    streams = [(qxt_ref[0, pair], km_ref.at[0], 2 * pair, vmt_ref.at[0], (2 * pair, 2 * pair + 1), N_MEM)
               for pair in range(N_XHEADS // 2)]
    pairs_per_group = N_HEADS // N_KV_HEADS // 2
    for group in range(N_KV_HEADS):
        q_t = jnp.concatenate(
            [qt_ref[0, group * pairs_per_group + p] for p in range(pairs_per_group)], axis=1)
        streams.append((q_t, k_ref.at[0], 2 * group, vt_ref.at[0], (group, group), KEY_CHUNK))
    outs = _attend(streams, ATTN_LOOKAHEAD)
    for pair in range(N_XHEADS // 2):
        put(outs[pair], POOL_W + ATTN_W + pair * LANES)
    for group in range(N_KV_HEADS):
        out = outs[N_XHEADS // 2 + group]
        for p in range(pairs_per_group):
            put(out[p * tq:(p + 1) * tq], POOL_W + (group * pairs_per_group + p) * LANES)

    y = _dot(mix_ref[...], wout_ref[...])
    ms = jnp.mean(y * y, axis=-1, keepdims=True)
    y_ref[0] = x_ref[0] + y * lax.rsqrt(ms + EPS) * gpost_ref[...]


def _mix(qt, qxt, k, vt, km, vmt, gate, u, x, wpool, pscale, wout, gpost):
    b, t, _ = x.shape
    tq = Q_ROWS
    n = t // tq
    halo_per_q = tq // POOL_HALO
    n_halo = t // POOL_HALO
    return pl.pallas_call(
        functools.partial(_mix_kernel, seq_len=t),
        grid=(b, n),
        in_specs=[
            pl.BlockSpec((1, N_HEADS // 2, LANES, tq), lambda i, j: (i, 0, 0, j)),
            pl.BlockSpec((1, N_XHEADS // 2, LANES, tq), lambda i, j: (i, 0, 0, j)),
            pl.BlockSpec((1, 2 * N_KV_HEADS, t, KV_W), lambda i, j: (i, 0, 0, 0)),
            pl.BlockSpec((1, N_KV_HEADS, VT_ROWS, t), lambda i, j: (i, 0, 0, 0)),
            pl.BlockSpec((1, N_XHEADS, N_MEM, LANES), lambda i, j: (i, 0, 0, 0)),
            pl.BlockSpec((1, N_XHEADS, VT_ROWS, N_MEM), lambda i, j: (i, 0, 0, 0)),
            pl.BlockSpec((1, tq, MIX_W), lambda i, j: (i, j, 0)),
            pl.BlockSpec((1, tq, POOL_W), lambda i, j: (i, j, 0)),
            pl.BlockSpec((1, POOL_HALO, POOL_W),
                         lambda i, j: (i, jnp.maximum(j * halo_per_q - 1, 0), 0)),
            pl.BlockSpec((1, POOL_HALO, POOL_W),
                         lambda i, j: (i, jnp.minimum((j + 1) * halo_per_q, n_halo - 1), 0)),
            pl.BlockSpec((1, tq, D_MODEL), lambda i, j: (i, j, 0)),
            pl.BlockSpec((POOL_W, POOL_W), lambda i, j: (0, 0)),
            pl.BlockSpec((1, POOL_W), lambda i, j: (0, 0)),
            pl.BlockSpec((MIX_W, D_MODEL), lambda i, j: (0, 0)),
            pl.BlockSpec((1, D_MODEL), lambda i, j: (0, 0)),
        ],
        out_specs=pl.BlockSpec((1, tq, D_MODEL), lambda i, j: (i, j, 0)),
        out_shape=jax.ShapeDtypeStruct((b, t, D_MODEL), _F32),
        scratch_shapes=[pltpu.VMEM((tq, MIX_W), _BF16)],
        compiler_params=pltpu.CompilerParams(
            dimension_semantics=("arbitrary", "arbitrary"), vmem_limit_bytes=VMEM_LIMIT_BYTES),
        name="mix",
    )(qt, qxt, k, vt, km, vmt, gate, u, u, u, x, wpool, pscale, wout, gpost)


def _rope_tables(t, gain, scale):
    rows = t // GRID_W
    row = jnp.repeat(jnp.arange(rows), GRID_W).astype(_F32)
    col = jnp.tile(jnp.arange(GRID_W), rows).astype(_F32)
    freqs = ROPE_THETA ** (-jnp.arange(ROPE_PAIRS, dtype=_F32) / ROPE_PAIRS)
    ang = jnp.stack([row[:, None] * freqs, col[:, None] * freqs], axis=1)
    cos, sin = jnp.cos(ang), jnp.sin(ang)
    cos_h = jnp.stack([cos, cos], axis=2).reshape(t, HEAD_DIM)
    sin_h = jnp.stack([-sin, sin], axis=2).reshape(t, HEAD_DIM)
    g = gain.astype(_F32)
    g_sw = g.reshape(2, 2, ROPE_PAIRS)[:, ::-1, :].reshape(HEAD_DIM)
    cos_h = cos_h * (g * scale)
    sin_h = sin_h * (g_sw * scale)
    return jnp.tile(cos_h, (1, LANES // HEAD_DIM)), jnp.tile(sin_h, (1, LANES // HEAD_DIM))


def _block_diag(blocks):
    n = len(blocks)
    rows = []
    for i, blk in enumerate(blocks):
        rows.append(jnp.concatenate(
            [blk if j == i else jnp.zeros_like(blk) for j in range(n)], axis=1))
    return jnp.concatenate(rows, axis=0)


def _trunk(x, mem, norm_pre, norm_post, w_in, pool_w, pool_scale, q_norm, k_norm, mem_norm,
           w_mem_kv, w_out):
    t = x.shape[1]
    ones = jnp.ones((HEAD_DIM, HEAD_DIM), _BF16)
    seg = _block_diag([ones, ones])
    for l in range(DEPTH):
        cq, sq = _rope_tables(t, q_norm[l], Q_SCALE)
        ck, sk = _rope_tables(t, k_norm[l], 1.0)
        km, vmt = _mem_kv(mem, mem_norm[l][None], w_mem_kv[l].astype(_BF16))
        u, gate, qt, qxt, k, vt = _proj(x, norm_pre[l][None], w_in[l].astype(_BF16), seg,
                                        cq, sq, ck, sk)
        wpool = _block_diag([pool_w[l, g] for g in range(len(POOL_WINDOWS))]).astype(_BF16)
        x = _mix(qt, qxt, k, vt, km, vmt, gate, u, x, wpool, pool_scale[l][None],
                 w_out[l].astype(_BF16), norm_post[l][None])
    return x


def kernel(x_prompt, x_sample, mem_prompt, mem_sample, norm_pre, norm_post, w_in, pool_w, pool_scale,
           q_norm, k_norm, mem_norm, w_mem_kv, w_out):
    weights = (norm_pre, norm_post, w_in, pool_w, pool_scale, q_norm, k_norm, mem_norm, w_mem_kv, w_out)
    return (_trunk(x_prompt, mem_prompt, *weights), _trunk(x_sample, mem_sample, *weights))
```

```python
import functools
import math

import jax
import jax.numpy as jnp
from jax import lax
from jax.experimental import pallas as pl
from jax.experimental.pallas import tpu as pltpu

D_MODEL = 1024
DEPTH = 2
GRID_W = 64
N_MEM = 256
HEAD_DIM = 64
N_HEADS = 8
N_KV_HEADS = 2
ATTN_W = N_HEADS * HEAD_DIM
KV_W = N_KV_HEADS * HEAD_DIM
POOL_W = 256
POOL_WINDOWS = (2, 4, 8, 16)
POOL_GROUP = POOL_W // len(POOL_WINDOWS)
N_XHEADS = 4
XATTN_W = N_XHEADS * HEAD_DIM
MIX_W = POOL_W + ATTN_W + XATTN_W
IN_W = 2 * POOL_W + 2 * ATTN_W + 2 * KV_W + 2 * XATTN_W
ROPE_THETA = 10000.0
ROPE_PAIRS = HEAD_DIM // 4
EPS = 1e-6
Q_SCALE = math.log2(math.e) * HEAD_DIM ** -0.5

LANES = 128
MXU_WIDTH = 256
BF16_SUBLANES = 16
POOL_HALO = 8
PROJ_ROWS = 1024
KEY_CHUNK = 256
Q_ROWS = 256
Q_SUB = 256
PAIRS_PER_STREAM = 2
ATTN_LOOKAHEAD = 2
VMEM_LIMIT_BYTES = 56 * 1024 * 1024

VT_ROWS = HEAD_DIM + BF16_SUBLANES

_OFF_U = 0
_OFF_GP = POOL_W
_OFF_Q = 2 * POOL_W
_OFF_K = _OFF_Q + ATTN_W
_OFF_V = _OFF_K + KV_W
_OFF_GA = _OFF_V + KV_W
_OFF_QX = _OFF_GA + ATTN_W
_OFF_GX = _OFF_QX + XATTN_W

_BF16 = jnp.bfloat16
_F32 = jnp.float32


def _dot(a, b):
    return jnp.dot(a, b, preferred_element_type=_F32)


def _silu(g):
    return g * (1.0 / (1.0 + jnp.exp(-g)))


def _is_even_head_lane(shape):
    return (lax.broadcasted_iota(jnp.int32, shape, len(shape) - 1) % LANES) < HEAD_DIM


def _ones_row_tile(cols):
    row = lax.broadcasted_iota(jnp.int32, (BF16_SUBLANES, cols), 0)
    return jnp.where(row == 0, 1.0, 0.0).astype(_BF16)


def _mem_kv_kernel(mem_ref, g_ref, w_ref, k_ref, vt_ref):
    m = mem_ref[0]
    ms = jnp.mean(m * m, axis=-1, keepdims=True)
    mh = (m * lax.rsqrt(ms + EPS) * g_ref[...]).astype(_BF16)
    kv = _dot(mh, w_ref[...])
    k = kv[:, :XATTN_W]
    v_t = kv[:, XATTN_W:].T
    left = _is_even_head_lane((N_MEM, LANES))
    for pair in range(N_XHEADS // 2):
        kp = k[:, pair * LANES:(pair + 1) * LANES]
        k_ref[0, 2 * pair] = jnp.where(left, kp, 0.0).astype(_BF16)
        k_ref[0, 2 * pair + 1] = jnp.where(left, 0.0, kp).astype(_BF16)
    for head in range(N_XHEADS):
        vt_ref[0, head, 0:HEAD_DIM] = v_t[head * HEAD_DIM:(head + 1) * HEAD_DIM].astype(_BF16)
        vt_ref[0, head, HEAD_DIM:VT_ROWS] = _ones_row_tile(N_MEM)


def _mem_kv(mem, g, w):
    b = mem.shape[0]
    return pl.pallas_call(
        _mem_kv_kernel,
        grid=(b,),
        in_specs=[
            pl.BlockSpec((1, N_MEM, D_MODEL), lambda i: (i, 0, 0)),
            pl.BlockSpec((1, D_MODEL), lambda i: (0, 0)),
            pl.BlockSpec((D_MODEL, 2 * XATTN_W), lambda i: (0, 0)),
        ],
        out_specs=[
            pl.BlockSpec((1, N_XHEADS, N_MEM, LANES), lambda i: (i, 0, 0, 0)),
            pl.BlockSpec((1, N_XHEADS, VT_ROWS, N_MEM), lambda i: (i, 0, 0, 0)),
        ],
        out_shape=[
            jax.ShapeDtypeStruct((b, N_XHEADS, N_MEM, LANES), _BF16),
            jax.ShapeDtypeStruct((b, N_XHEADS, VT_ROWS, N_MEM), _BF16),
        ],
        compiler_params=pltpu.CompilerParams(
            dimension_semantics=("arbitrary",), vmem_limit_bytes=VMEM_LIMIT_BYTES),
        name="mem_kv",
    )(mem, g, w)


def _swap16(x):
    n = x.shape[-1]
    lane = lax.broadcasted_iota(jnp.int32, x.shape, 1)
    first = (lane & ROPE_PAIRS) == 0
    return jnp.where(first, pltpu.roll(x, n - ROPE_PAIRS, 1), pltpu.roll(x, ROPE_PAIRS, 1))


def _norm_rope(z, seg, cos_tab, sin_tab):
    reps = z.shape[1] // LANES
    cos_tab = jnp.concatenate([cos_tab] * reps, axis=1)
    sin_tab = jnp.concatenate([sin_tab] * reps, axis=1)
    ssq = _dot((z * z).astype(_BF16), seg)
    r = lax.rsqrt(ssq * (1.0 / HEAD_DIM) + EPS)
    return r * (z * cos_tab + _swap16(z) * sin_tab)


def _proj_kernel(x_ref, g_ref, w_ref, seg_ref, cq_ref, sq_ref, ck_ref, sk_ref,
                 u_ref, gate_ref, qt_ref, qxt_ref, k_ref, vt_ref):
    x = x_ref[0]
    rows = x.shape[0]
    ms = jnp.mean(x * x, axis=-1, keepdims=True)
    h = (x * lax.rsqrt(ms + EPS) * g_ref[...]).astype(_BF16)

    z_qkv = _dot(h, w_ref[:, _OFF_Q:_OFF_GA])
    z_pool = _dot(h, w_ref[:, _OFF_U:_OFF_Q])

    seg = seg_ref[...]
    cq, sq = cq_ref[...], sq_ref[...]
    seg_w = seg.shape[0]
    for lo in range(0, ATTN_W, seg_w):
        q = _norm_rope(z_qkv[:, lo:lo + seg_w], seg, cq, sq)
        for pair in range(lo // LANES, (lo + seg_w) // LANES):
            qt_ref[0, pair] = q[:, pair * LANES - lo:(pair + 1) * LANES - lo].T.astype(_BF16)

    k = _norm_rope(z_qkv[:, ATTN_W:ATTN_W + KV_W], seg[:KV_W, :KV_W], ck_ref[...], sk_ref[...])
    k_sw = pltpu.roll(k, HEAD_DIM, 1)
    left = _is_even_head_lane(k.shape)
    k_ref[0, 0] = jnp.where(left, k, 0.0).astype(_BF16)
    k_ref[0, 1] = jnp.where(left, 0.0, k_sw).astype(_BF16)
    k_ref[0, 2] = jnp.where(left, k_sw, 0.0).astype(_BF16)
    k_ref[0, 3] = jnp.where(left, 0.0, k).astype(_BF16)

    v_t = z_qkv[:, ATTN_W + KV_W:].T
    for head in range(N_KV_HEADS):
        vt_ref[0, head, 0:HEAD_DIM] = v_t[head * HEAD_DIM:(head + 1) * HEAD_DIM].astype(_BF16)
        vt_ref[0, head, HEAD_DIM:VT_ROWS] = _ones_row_tile(rows)

    u_ref[0] = z_pool[:, 0:POOL_W]
    gate_ref[0, :, 0:POOL_W] = _silu(z_pool[:, POOL_W:]).astype(_BF16)

    z_rest = _dot(h, w_ref[:, _OFF_GA:])
    gate_ref[0, :, POOL_W:POOL_W + ATTN_W] = _silu(z_rest[:, 0:ATTN_W]).astype(_BF16)
    gate_ref[0, :, POOL_W + ATTN_W:MIX_W] = _silu(z_rest[:, ATTN_W + XATTN_W:]).astype(_BF16)
    for pair in range(N_XHEADS // 2):
        lo = ATTN_W + pair * LANES
        qxt_ref[0, pair] = (z_rest[:, lo:lo + LANES] * Q_SCALE).T.astype(_BF16)


def _proj(x, g, w, seg, cq, sq, ck, sk):
    b, t, _ = x.shape
    rows = PROJ_ROWS
    n = t // rows
    tab = pl.BlockSpec((rows, LANES), lambda i, j: (j, 0))
    return pl.pallas_call(
        _proj_kernel,
        grid=(b, n),
        in_specs=[
            pl.BlockSpec((1, rows, D_MODEL), lambda i, j: (i, j, 0)),
            pl.BlockSpec((1, D_MODEL), lambda i, j: (0, 0)),
            pl.BlockSpec((D_MODEL, IN_W), lambda i, j: (0, 0)),
            pl.BlockSpec((MXU_WIDTH, MXU_WIDTH), lambda i, j: (0, 0)),
            tab, tab, tab, tab,
        ],
        out_specs=[
            pl.BlockSpec((1, rows, POOL_W), lambda i, j: (i, j, 0)),
            pl.BlockSpec((1, rows, MIX_W), lambda i, j: (i, j, 0)),
            pl.BlockSpec((1, N_HEADS // 2, LANES, rows), lambda i, j: (i, 0, 0, j)),
            pl.BlockSpec((1, N_XHEADS // 2, LANES, rows), lambda i, j: (i, 0, 0, j)),
            pl.BlockSpec((1, 2 * N_KV_HEADS, rows, KV_W), lambda i, j: (i, 0, j, 0)),
            pl.BlockSpec((1, N_KV_HEADS, VT_ROWS, rows), lambda i, j: (i, 0, 0, j)),
        ],
        out_shape=[
            jax.ShapeDtypeStruct((b, t, POOL_W), _F32),
            jax.ShapeDtypeStruct((b, t, MIX_W), _BF16),
            jax.ShapeDtypeStruct((b, N_HEADS // 2, LANES, t), _BF16),
            jax.ShapeDtypeStruct((b, N_XHEADS // 2, LANES, t), _BF16),
            jax.ShapeDtypeStruct((b, 2 * N_KV_HEADS, t, KV_W), _BF16),
            jax.ShapeDtypeStruct((b, N_KV_HEADS, VT_ROWS, t), _BF16),
        ],
        compiler_params=pltpu.CompilerParams(
            dimension_semantics=("arbitrary", "arbitrary"), vmem_limit_bytes=VMEM_LIMIT_BYTES),
        name="proj",
    )(x, g, w, seg, cq, sq, ck, sk)


def _attend(streams, lookahead):
    n_chunks = [s[1].shape[1] // s[5] for s in streams]
    units = [(c, i, parity) for c in range(max(n_chunks)) for i in range(len(streams))
             if c < n_chunks[i] for parity in range(2)]
    units = [u for u in units if n_chunks[u[1]] > 1] + [u for u in units if n_chunks[u[1]] == 1]
    state = [[None, None] for _ in streams]
    scores = {}

    def issue_scores(c, i, parity):
        q_t, k_ref, k_slot = streams[i][:3]
        chunk = streams[i][5]
        return _dot(k_ref[k_slot + parity, c * chunk:(c + 1) * chunk, :], q_t)

    def finish(c, i, parity, s):
        _, _, _, vt_ref, v_slots, chunk, sink = streams[i]
        mn = jnp.max(s, axis=0, keepdims=True)
        if state[i][parity] is not None:
            m, acc = state[i][parity]
            mn = jnp.maximum(m, mn)
        p = jnp.exp2((s - mn).astype(_BF16))
        pv = _dot(vt_ref[v_slots[parity], :, c * chunk:(c + 1) * chunk], p)
        if state[i][parity] is not None:
            pv = acc * jnp.exp2(m - mn) + pv
        state[i][parity] = (mn, pv)
        if c == n_chunks[i] - 1 and parity == 1:
            out_t = [acc[0:HEAD_DIM] * (1.0 / acc[HEAD_DIM:HEAD_DIM + 1]) for _, acc in state[i]]
            sink(jnp.concatenate(out_t, axis=0).T)

    for step in range(len(units) + lookahead):
        if step < len(units):
            scores[step] = issue_scores(*units[step])
        if step >= lookahead:
            finish(*units[step - lookahead], scores.pop(step - lookahead))


def _pool_delta(u_ref, up_ref, un_ref, seq_len):
    tq = u_ref.shape[1]
    n_ext = tq + 2 * POOL_HALO
    t0 = pl.program_id(1) * tq
    t_ext = t0 - POOL_HALO + lax.broadcasted_iota(jnp.int32, (n_ext, LANES), 0)
    in_seq = (t_ext >= 0) & (t_ext < seq_len)
    t = t0 + lax.broadcasted_iota(jnp.int32, (tq, LANES), 0)
    left = _is_even_head_lane((tq, LANES))
    deltas = []
    for half in range(POOL_W // LANES):
        w_a, w_b = POOL_WINDOWS[2 * half], POOL_WINDOWS[2 * half + 1]
        cols = slice(half * LANES, (half + 1) * LANES)
        u = u_ref[0, :, cols]
        ext = jnp.concatenate([up_ref[0, :, cols], u, un_ref[0, :, cols]], axis=0)
        ext = jnp.where(in_seq, ext, 0.0)
        sums = {}
        run = ext
        for w in POOL_WINDOWS[:2 * half + 2]:
            run = run + pltpu.roll(run, n_ext - w // 2, 0)
            if w in (w_a, w_b):
                aligned = run if w // 2 == POOL_HALO else pltpu.roll(run, w // 2, 0)
                first = 0 if w // 2 == POOL_HALO else POOL_HALO
                sums[w] = aligned[first:first + tq]
        reach = jnp.where(left, w_a // 2, w_b // 2)
        cnt = jnp.minimum(t + reach, seq_len) - jnp.maximum(t - reach, 0)
        mean = jnp.where(left, sums[w_a], sums[w_b]) * (1.0 / cnt.astype(_F32))
        deltas.append(mean - u)
    return jnp.concatenate(deltas, axis=1)


def _mix_kernel(qt_ref, qxt_ref, k_ref, vt_ref, km_ref, vmt_ref, gate_ref, u_ref, up_ref, un_ref,
                x_ref, wpool_ref, pscale_ref, wout_ref, gpost_ref, y_ref, mix_ref, *, seq_len):
    d = _pool_delta(u_ref, up_ref, un_ref, seq_len).astype(_BF16)
    pool = _dot(d, wpool_ref[...]) * pscale_ref[...]
    mix_ref[:, 0:POOL_W] = (pool * gate_ref[0, :, 0:POOL_W].astype(_F32)).astype(_BF16)

    tq = qt_ref.shape[3]
    pairs_per_group = N_HEADS // N_KV_HEADS // 2

    def gated_sink(r0, first_cols):
        def sink(out):
            for n, c0 in enumerate(first_cols):
                gate = gate_ref[0, r0:r0 + Q_SUB, c0:c0 + LANES].astype(_F32)
                mix_ref[r0:r0 + Q_SUB, c0:c0 + LANES] = (out[n * Q_SUB:(n + 1) * Q_SUB] * gate).astype(_BF16)
        return sink

    streams = []
    for r0 in range(0, tq, Q_SUB):
        cols = slice(r0, r0 + Q_SUB)
        for pair in range(N_XHEADS // 2):
            streams.append((qxt_ref[0, pair, :, cols], km_ref.at[0], 2 * pair,
                            vmt_ref.at[0], (2 * pair, 2 * pair + 1), N_MEM,
                            gated_sink(r0, [POOL_W + ATTN_W + pair * LANES])))
        for first in range(0, N_HEADS // 2, PAIRS_PER_STREAM):
            group = first // pairs_per_group
            pairs = list(range(first, first + PAIRS_PER_STREAM))
            q_t = jnp.concatenate([qt_ref[0, pair, :, cols] for pair in pairs], axis=1)
            streams.append((q_t, k_ref.at[0], 2 * group, vt_ref.at[0], (group, group), KEY_CHUNK,
                            gated_sink(r0, [POOL_W + pair * LANES for pair in pairs])))

    _attend(streams, ATTN_LOOKAHEAD)

    y = _dot(mix_ref[...], wout_ref[...])
    ms = jnp.mean(y * y, axis=-1, keepdims=True)
    y_ref[0] = x_ref[0] + y * lax.rsqrt(ms + EPS) * gpost_ref[...]


def _mix(qt, qxt, k, vt, km, vmt, gate, u, x, wpool, pscale, wout, gpost):
    b, t, _ = x.shape
    tq = Q_ROWS
    n = t // tq
    halo_per_q = tq // POOL_HALO
    n_halo = t // POOL_HALO
    return pl.pallas_call(
        functools.partial(_mix_kernel, seq_len=t),
        grid=(b, n),
        in_specs=[
            pl.BlockSpec((1, N_HEADS // 2, LANES, tq), lambda i, j: (i, 0, 0, j)),
            pl.BlockSpec((1, N_XHEADS // 2, LANES, tq), lambda i, j: (i, 0, 0, j)),
            pl.BlockSpec((1, 2 * N_KV_HEADS, t, KV_W), lambda i, j: (i, 0, 0, 0)),
            pl.BlockSpec((1, N_KV_HEADS, VT_ROWS, t), lambda i, j: (i, 0, 0, 0)),
            pl.BlockSpec((1, N_XHEADS, N_MEM, LANES), lambda i, j: (i, 0, 0, 0)),
            pl.BlockSpec((1, N_XHEADS, VT_ROWS, N_MEM), lambda i, j: (i, 0, 0, 0)),
            pl.BlockSpec((1, tq, MIX_W), lambda i, j: (i, j, 0)),
            pl.BlockSpec((1, tq, POOL_W), lambda i, j: (i, j, 0)),
            pl.BlockSpec((1, POOL_HALO, POOL_W),
                         lambda i, j: (i, jnp.maximum(j * halo_per_q - 1, 0), 0)),
            pl.BlockSpec((1, POOL_HALO, POOL_W),
                         lambda i, j: (i, jnp.minimum((j + 1) * halo_per_q, n_halo - 1), 0)),
            pl.BlockSpec((1, tq, D_MODEL), lambda i, j: (i, j, 0)),
            pl.BlockSpec((POOL_W, POOL_W), lambda i, j: (0, 0)),
            pl.BlockSpec((1, POOL_W), lambda i, j: (0, 0)),
            pl.BlockSpec((MIX_W, D_MODEL), lambda i, j: (0, 0)),
            pl.BlockSpec((1, D_MODEL), lambda i, j: (0, 0)),
        ],
        out_specs=pl.BlockSpec((1, tq, D_MODEL), lambda i, j: (i, j, 0)),
        out_shape=jax.ShapeDtypeStruct((b, t, D_MODEL), _F32),
        scratch_shapes=[pltpu.VMEM((tq, MIX_W), _BF16)],
        compiler_params=pltpu.CompilerParams(
            dimension_semantics=("arbitrary", "arbitrary"), vmem_limit_bytes=VMEM_LIMIT_BYTES),
        name="mix",
    )(qt, qxt, k, vt, km, vmt, gate, u, u, u, x, wpool, pscale, wout, gpost)


def _rope_tables(t, gain, scale):
    rows = t // GRID_W
    row = jnp.repeat(jnp.arange(rows), GRID_W).astype(_F32)
    col = jnp.tile(jnp.arange(GRID_W), rows).astype(_F32)
    freqs = ROPE_THETA ** (-jnp.arange(ROPE_PAIRS, dtype=_F32) / ROPE_PAIRS)
    ang = jnp.stack([row[:, None] * freqs, col[:, None] * freqs], axis=1)
    cos, sin = jnp.cos(ang), jnp.sin(ang)
    cos_h = jnp.stack([cos, cos], axis=2).reshape(t, HEAD_DIM)
    sin_h = jnp.stack([-sin, sin], axis=2).reshape(t, HEAD_DIM)
    g = gain.astype(_F32)
    g_sw = g.reshape(2, 2, ROPE_PAIRS)[:, ::-1, :].reshape(HEAD_DIM)
    cos_h = cos_h * (g * scale)
    sin_h = sin_h * (g_sw * scale)
    return jnp.tile(cos_h, (1, LANES // HEAD_DIM)), jnp.tile(sin_h, (1, LANES // HEAD_DIM))


def _block_diag(blocks):
    n = len(blocks)
    rows = []
    for i, blk in enumerate(blocks):
        rows.append(jnp.concatenate(
            [blk if j == i else jnp.zeros_like(blk) for j in range(n)], axis=1))
    return jnp.concatenate(rows, axis=0)


def _trunk(x, mem, norm_pre, norm_post, w_in, pool_w, pool_scale, q_norm, k_norm, mem_norm,
           w_mem_kv, w_out):
    t = x.shape[1]
    ones = jnp.ones((HEAD_DIM, HEAD_DIM), _BF16)
    seg = _block_diag([ones] * (MXU_WIDTH // HEAD_DIM))
    for l in range(DEPTH):
        cq, sq = _rope_tables(t, q_norm[l], Q_SCALE)
        ck, sk = _rope_tables(t, k_norm[l], 1.0)
        km, vmt = _mem_kv(mem, mem_norm[l][None], w_mem_kv[l].astype(_BF16))
        u, gate, qt, qxt, k, vt = _proj(x, norm_pre[l][None], w_in[l].astype(_BF16), seg,
                                        cq, sq, ck, sk)
        wpool = _block_diag([pool_w[l, g] for g in range(len(POOL_WINDOWS))]).astype(_BF16)
        x = _mix(qt, qxt, k, vt, km, vmt, gate, u, x, wpool, pool_scale[l][None],
                 w_out[l].astype(_BF16), norm_post[l][None])
    return x


def kernel(x_prompt, x_sample, mem_prompt, mem_sample, norm_pre, norm_post, w_in, pool_w, pool_scale,
           q_norm, k_norm, mem_norm, w_mem_kv, w_out):
    weights = (norm_pre, norm_post, w_in, pool_w, pool_scale, q_norm, k_norm, mem_norm, w_mem_kv, w_out)
    return (_trunk(x_prompt, mem_prompt, *weights), _trunk(x_sample, mem_sample, *weights))
```

```python
import functools
import math

import jax
import jax.numpy as jnp
from jax import lax
from jax.experimental import pallas as pl
from jax.experimental.pallas import tpu as pltpu

D_MODEL = 1024
DEPTH = 2
GRID_W = 64
N_MEM = 256
HEAD_DIM = 64
N_HEADS = 8
N_KV_HEADS = 2
ATTN_W = N_HEADS * HEAD_DIM
KV_W = N_KV_HEADS * HEAD_DIM
POOL_W = 256
POOL_WINDOWS = (2, 4, 8, 16)
POOL_GROUP = POOL_W // len(POOL_WINDOWS)
N_XHEADS = 4
XATTN_W = N_XHEADS * HEAD_DIM
MIX_W = POOL_W + ATTN_W + XATTN_W
IN_W = 2 * POOL_W + 2 * ATTN_W + 2 * KV_W + 2 * XATTN_W
ROPE_THETA = 10000.0
ROPE_PAIRS = HEAD_DIM // 4
EPS = 1e-6
Q_SCALE = math.log2(math.e) * HEAD_DIM ** -0.5

LANES = 128
MXU_WIDTH = 256
BF16_SUBLANES = 16
POOL_HALO = 8
PROJ_ROWS = 1024
KEY_CHUNK = 256
Q_ROWS = 512
Q_SUB = 256
PAIRS_PER_STREAM = 2
ATTN_LOOKAHEAD = 2
VMEM_LIMIT_BYTES = 56 * 1024 * 1024

VT_ROWS = HEAD_DIM + BF16_SUBLANES

_OFF_U = 0
_OFF_GP = POOL_W
_OFF_Q = 2 * POOL_W
_OFF_K = _OFF_Q + ATTN_W
_OFF_V = _OFF_K + KV_W
_OFF_GA = _OFF_V + KV_W
_OFF_QX = _OFF_GA + ATTN_W
_OFF_GX = _OFF_QX + XATTN_W

_BF16 = jnp.bfloat16
_F32 = jnp.float32


def _dot(a, b):
    return jnp.dot(a, b, preferred_element_type=_F32)


def _silu(g):
    return g * (1.0 / (1.0 + jnp.exp(-g)))


def _is_even_head_lane(shape):
    return (lax.broadcasted_iota(jnp.int32, shape, len(shape) - 1) % LANES) < HEAD_DIM


def _ones_row_tile(cols):
    row = lax.broadcasted_iota(jnp.int32, (BF16_SUBLANES, cols), 0)
    return jnp.where(row == 0, 1.0, 0.0).astype(_BF16)


def _memory_kv(mem_ref, g_ref, w_ref, k_ref, vt_ref):
    m = mem_ref[0]
    ms = jnp.mean(m * m, axis=-1, keepdims=True)
    mh = (m * lax.rsqrt(ms + EPS) * g_ref[...]).astype(_BF16)
    kv = _dot(mh, w_ref[...])
    k = kv[:, :XATTN_W]
    v_t = kv[:, XATTN_W:].T
    left = _is_even_head_lane((N_MEM, LANES))
    for pair in range(N_XHEADS // 2):
        kp = k[:, pair * LANES:(pair + 1) * LANES]
        k_ref[2 * pair] = jnp.where(left, kp, 0.0).astype(_BF16)
        k_ref[2 * pair + 1] = jnp.where(left, 0.0, kp).astype(_BF16)
    for head in range(N_XHEADS):
        vt_ref[head, 0:HEAD_DIM] = v_t[head * HEAD_DIM:(head + 1) * HEAD_DIM].astype(_BF16)
        vt_ref[head, HEAD_DIM:VT_ROWS] = _ones_row_tile(N_MEM)


def _swap16(x):
    n = x.shape[-1]
    lane = lax.broadcasted_iota(jnp.int32, x.shape, 1)
    first = (lane & ROPE_PAIRS) == 0
    return jnp.where(first, pltpu.roll(x, n - ROPE_PAIRS, 1), pltpu.roll(x, ROPE_PAIRS, 1))


def _norm_rope(z, seg, cos_tab, sin_tab):
    reps = z.shape[1] // LANES
    cos_tab = jnp.concatenate([cos_tab] * reps, axis=1)
    sin_tab = jnp.concatenate([sin_tab] * reps, axis=1)
    ssq = _dot((z * z).astype(_BF16), seg)
    r = lax.rsqrt(ssq * (1.0 / HEAD_DIM) + EPS)
    return r * (z * cos_tab + _swap16(z) * sin_tab)


def _proj_kernel(x_ref, g_ref, w_ref, seg_ref, cq_ref, sq_ref, ck_ref, sk_ref,
                 u_ref, gate_ref, qt_ref, qxt_ref, k_ref, vt_ref):
    x = x_ref[0]
    rows = x.shape[0]
    ms = jnp.mean(x * x, axis=-1, keepdims=True)
    h = (x * lax.rsqrt(ms + EPS) * g_ref[...]).astype(_BF16)

    z_qkv = _dot(h, w_ref[:, _OFF_Q:_OFF_GA])
    z_pool = _dot(h, w_ref[:, _OFF_U:_OFF_Q])

    seg = seg_ref[...]
    cq, sq = cq_ref[...], sq_ref[...]
    seg_w = seg.shape[0]
    for lo in range(0, ATTN_W, seg_w):
        q = _norm_rope(z_qkv[:, lo:lo + seg_w], seg, cq, sq)
        for pair in range(lo // LANES, (lo + seg_w) // LANES):
            qt_ref[0, pair] = q[:, pair * LANES - lo:(pair + 1) * LANES - lo].T.astype(_BF16)

    k = _norm_rope(z_qkv[:, ATTN_W:ATTN_W + KV_W], seg[:KV_W, :KV_W], ck_ref[...], sk_ref[...])
    k_sw = pltpu.roll(k, HEAD_DIM, 1)
    left = _is_even_head_lane(k.shape)
    k_ref[0, 0] = jnp.where(left, k, 0.0).astype(_BF16)
    k_ref[0, 1] = jnp.where(left, 0.0, k_sw).astype(_BF16)
    k_ref[0, 2] = jnp.where(left, k_sw, 0.0).astype(_BF16)
    k_ref[0, 3] = jnp.where(left, 0.0, k).astype(_BF16)

    v_t = z_qkv[:, ATTN_W + KV_W:].T
    for head in range(N_KV_HEADS):
        vt_ref[0, head, 0:HEAD_DIM] = v_t[head * HEAD_DIM:(head + 1) * HEAD_DIM].astype(_BF16)
        vt_ref[0, head, HEAD_DIM:VT_ROWS] = _ones_row_tile(rows)

    u_ref[0] = z_pool[:, 0:POOL_W]
    gate_ref[0, :, 0:POOL_W] = _silu(z_pool[:, POOL_W:]).astype(_BF16)

    z_rest = _dot(h, w_ref[:, _OFF_GA:])
    gate_ref[0, :, POOL_W:POOL_W + ATTN_W] = _silu(z_rest[:, 0:ATTN_W]).astype(_BF16)
    gate_ref[0, :, POOL_W + ATTN_W:MIX_W] = _silu(z_rest[:, ATTN_W + XATTN_W:]).astype(_BF16)
    for pair in range(N_XHEADS // 2):
        lo = ATTN_W + pair * LANES
        qxt_ref[0, pair] = (z_rest[:, lo:lo + LANES] * Q_SCALE).T.astype(_BF16)


def _proj(layer, x, g, w, seg, cq, sq, ck, sk):
    b, t, _ = x.shape
    rows = PROJ_ROWS
    n = t // rows
    tab = pl.BlockSpec((None, rows, LANES), lambda i, j: (layer, j, 0))
    return pl.pallas_call(
        _proj_kernel,
        grid=(b, n),
        in_specs=[
            pl.BlockSpec((1, rows, D_MODEL), lambda i, j: (i, j, 0)),
            pl.BlockSpec((None, 1, D_MODEL), lambda i, j: (layer, 0, 0)),
            pl.BlockSpec((None, D_MODEL, IN_W), lambda i, j: (layer, 0, 0)),
            pl.BlockSpec((MXU_WIDTH, MXU_WIDTH), lambda i, j: (0, 0)),
            tab, tab, tab, tab,
        ],
        out_specs=[
            pl.BlockSpec((1, rows, POOL_W), lambda i, j: (i, j, 0)),
            pl.BlockSpec((1, rows, MIX_W), lambda i, j: (i, j, 0)),
            pl.BlockSpec((1, N_HEADS // 2, LANES, rows), lambda i, j: (i, 0, 0, j)),
            pl.BlockSpec((1, N_XHEADS // 2, LANES, rows), lambda i, j: (i, 0, 0, j)),
            pl.BlockSpec((1, 2 * N_KV_HEADS, rows, KV_W), lambda i, j: (i, 0, j, 0)),
            pl.BlockSpec((1, N_KV_HEADS, VT_ROWS, rows), lambda i, j: (i, 0, 0, j)),
        ],
        out_shape=[
            jax.ShapeDtypeStruct((b, t, POOL_W), _F32),
            jax.ShapeDtypeStruct((b, t, MIX_W), _BF16),
            jax.ShapeDtypeStruct((b, N_HEADS // 2, LANES, t), _BF16),
            jax.ShapeDtypeStruct((b, N_XHEADS // 2, LANES, t), _BF16),
            jax.ShapeDtypeStruct((b, 2 * N_KV_HEADS, t, KV_W), _BF16),
            jax.ShapeDtypeStruct((b, N_KV_HEADS, VT_ROWS, t), _BF16),
        ],
        compiler_params=pltpu.CompilerParams(
            dimension_semantics=("arbitrary", "arbitrary"), vmem_limit_bytes=VMEM_LIMIT_BYTES),
        name="proj",
    )(x, g, w, seg, cq, sq, ck, sk)


def _attend(streams, lookahead):
    n_chunks = [s[1].shape[1] // s[5] for s in streams]
    units = [(c, i, parity) for c in range(max(n_chunks)) for i in range(len(streams))
             if c < n_chunks[i] for parity in range(2)]
    units = [u for u in units if n_chunks[u[1]] > 1] + [u for u in units if n_chunks[u[1]] == 1]
    state = [[None, None] for _ in streams]
    scores = {}

    def issue_scores(c, i, parity):
        q_t, k_ref, k_slot = streams[i][:3]
        chunk = streams[i][5]
        return _dot(k_ref[k_slot + parity, c * chunk:(c + 1) * chunk, :], q_t)

    def finish(c, i, parity, s):
        _, _, _, vt_ref, v_slots, chunk, sink = streams[i]
        mn = jnp.max(s, axis=0, keepdims=True)
        if state[i][parity] is not None:
            m, acc = state[i][parity]
            mn = jnp.maximum(m, mn)
        p = jnp.exp2((s - mn).astype(_BF16))
        pv = _dot(vt_ref[v_slots[parity], :, c * chunk:(c + 1) * chunk], p)
        if state[i][parity] is not None:
            pv = acc * jnp.exp2(m - mn) + pv
        state[i][parity] = (mn, pv)
        if c == n_chunks[i] - 1 and parity == 1:
            out_t = [acc[0:HEAD_DIM] * (1.0 / acc[HEAD_DIM:HEAD_DIM + 1]) for _, acc in state[i]]
            sink(jnp.concatenate(out_t, axis=0).T)

    for step in range(len(units) + lookahead):
        if step < len(units):
            scores[step] = issue_scores(*units[step])
        if step >= lookahead:
            finish(*units[step - lookahead], scores.pop(step - lookahead))


def _pool_delta(u_ref, up_ref, un_ref, seq_len):
    tq = u_ref.shape[1]
    n_ext = tq + 2 * POOL_HALO
    t0 = pl.program_id(1) * tq
    t_ext = t0 - POOL_HALO + lax.broadcasted_iota(jnp.int32, (n_ext, LANES), 0)
    in_seq = (t_ext >= 0) & (t_ext < seq_len)
    t = t0 + lax.broadcasted_iota(jnp.int32, (tq, LANES), 0)
    left = _is_even_head_lane((tq, LANES))
    deltas = []
    for half in range(POOL_W // LANES):
        w_a, w_b = POOL_WINDOWS[2 * half], POOL_WINDOWS[2 * half + 1]
        cols = slice(half * LANES, (half + 1) * LANES)
        u = u_ref[0, :, cols]
        ext = jnp.concatenate([up_ref[0, :, cols], u, un_ref[0, :, cols]], axis=0)
        ext = jnp.where(in_seq, ext, 0.0)
        sums = {}
        run = ext
        for w in POOL_WINDOWS[:2 * half + 2]:
            run = run + pltpu.roll(run, n_ext - w // 2, 0)
            if w in (w_a, w_b):
                aligned = run if w // 2 == POOL_HALO else pltpu.roll(run, w // 2, 0)
                first = 0 if w // 2 == POOL_HALO else POOL_HALO
                sums[w] = aligned[first:first + tq]
        reach = jnp.where(left, w_a // 2, w_b // 2)
        cnt = jnp.minimum(t + reach, seq_len) - jnp.maximum(t - reach, 0)
        mean = jnp.where(left, sums[w_a], sums[w_b]) * (1.0 / cnt.astype(_F32))
        deltas.append(mean - u)
    return jnp.concatenate(deltas, axis=1)


def _mix_kernel(qt_ref, qxt_ref, k_ref, vt_ref, mem_ref, gmem_ref, wmem_ref, gate_ref, u_ref, up_ref,
                un_ref, x_ref, wpool_ref, pscale_ref, wout_ref, gpost_ref, y_ref,
                mix_ref, km_ref, vmt_ref, *, seq_len):
    @pl.when(pl.program_id(1) == 0)
    def _():
        _memory_kv(mem_ref, gmem_ref, wmem_ref, km_ref, vmt_ref)

    d = _pool_delta(u_ref, up_ref, un_ref, seq_len).astype(_BF16)
    pool = _dot(d, wpool_ref[...]) * pscale_ref[...]
    mix_ref[:, 0:POOL_W] = (pool * gate_ref[0, :, 0:POOL_W].astype(_F32)).astype(_BF16)

    tq = qt_ref.shape[3]
    pairs_per_group = N_HEADS // N_KV_HEADS // 2

    def gated_sink(r0, first_cols):
        def sink(out):
            for n, c0 in enumerate(first_cols):
                gate = gate_ref[0, r0:r0 + Q_SUB, c0:c0 + LANES].astype(_F32)
                mix_ref[r0:r0 + Q_SUB, c0:c0 + LANES] = (out[n * Q_SUB:(n + 1) * Q_SUB] * gate).astype(_BF16)
        return sink

    streams = []
    for r0 in range(0, tq, Q_SUB):
        cols = slice(r0, r0 + Q_SUB)
        for pair in range(N_XHEADS // 2):
            streams.append((qxt_ref[0, pair, :, cols], km_ref, 2 * pair,
                            vmt_ref, (2 * pair, 2 * pair + 1), N_MEM,
                            gated_sink(r0, [POOL_W + ATTN_W + pair * LANES])))
        for first in range(0, N_HEADS // 2, PAIRS_PER_STREAM):
            group = first // pairs_per_group
            pairs = list(range(first, first + PAIRS_PER_STREAM))
            q_t = jnp.concatenate([qt_ref[0, pair, :, cols] for pair in pairs], axis=1)
            streams.append((q_t, k_ref.at[0], 2 * group, vt_ref.at[0], (group, group), KEY_CHUNK,
                            gated_sink(r0, [POOL_W + pair * LANES for pair in pairs])))

    _attend(streams, ATTN_LOOKAHEAD)

    y = _dot(mix_ref[...], wout_ref[...])
    ms = jnp.mean(y * y, axis=-1, keepdims=True)
    y_ref[0] = x_ref[0] + y * lax.rsqrt(ms + EPS) * gpost_ref[...]


def _mix(layer, qt, qxt, k, vt, mem, gmem, wmem, gate, u, x, wpool, pscale, wout, gpost):
    b, t, _ = x.shape
    tq = Q_ROWS
    n = t // tq
    halo_per_q = tq // POOL_HALO
    n_halo = t // POOL_HALO
    return pl.pallas_call(
        functools.partial(_mix_kernel, seq_len=t),
        grid=(b, n),
        in_specs=[
            pl.BlockSpec((1, N_HEADS // 2, LANES, tq), lambda i, j: (i, 0, 0, j)),
            pl.BlockSpec((1, N_XHEADS // 2, LANES, tq), lambda i, j: (i, 0, 0, j)),
            pl.BlockSpec((1, 2 * N_KV_HEADS, t, KV_W), lambda i, j: (i, 0, 0, 0)),
            pl.BlockSpec((1, N_KV_HEADS, VT_ROWS, t), lambda i, j: (i, 0, 0, 0)),
            pl.BlockSpec((1, N_MEM, D_MODEL), lambda i, j: (i, 0, 0)),
            pl.BlockSpec((None, 1, D_MODEL), lambda i, j: (layer, 0, 0)),
            pl.BlockSpec((None, D_MODEL, 2 * XATTN_W), lambda i, j: (layer, 0, 0)),
            pl.BlockSpec((1, tq, MIX_W), lambda i, j: (i, j, 0)),
            pl.BlockSpec((1, tq, POOL_W), lambda i, j: (i, j, 0)),
            pl.BlockSpec((1, POOL_HALO, POOL_W),
                         lambda i, j: (i, jnp.maximum(j * halo_per_q - 1, 0), 0)),
            pl.BlockSpec((1, POOL_HALO, POOL_W),
                         lambda i, j: (i, jnp.minimum((j + 1) * halo_per_q, n_halo - 1), 0)),
            pl.BlockSpec((1, tq, D_MODEL), lambda i, j: (i, j, 0)),
            pl.BlockSpec((None, POOL_W, POOL_W), lambda i, j: (layer, 0, 0)),
            pl.BlockSpec((None, 1, POOL_W), lambda i, j: (layer, 0, 0)),
            pl.BlockSpec((None, MIX_W, D_MODEL), lambda i, j: (layer, 0, 0)),
            pl.BlockSpec((None, 1, D_MODEL), lambda i, j: (layer, 0, 0)),
        ],
        out_specs=pl.BlockSpec((1, tq, D_MODEL), lambda i, j: (i, j, 0)),
        out_shape=jax.ShapeDtypeStruct((b, t, D_MODEL), _F32),
        scratch_shapes=[
            pltpu.VMEM((tq, MIX_W), _BF16),
            pltpu.VMEM((N_XHEADS, N_MEM, LANES), _BF16),
            pltpu.VMEM((N_XHEADS, VT_ROWS, N_MEM), _BF16),
        ],
        compiler_params=pltpu.CompilerParams(
            dimension_semantics=("arbitrary", "arbitrary"), vmem_limit_bytes=VMEM_LIMIT_BYTES),
        name="mix",
    )(qt, qxt, k, vt, mem, gmem, wmem, gate, u, u, u, x, wpool, pscale, wout, gpost)


def _rope_base(t):
    rows = t // GRID_W
    row = jnp.repeat(jnp.arange(rows), GRID_W).astype(_F32)
    col = jnp.tile(jnp.arange(GRID_W), rows).astype(_F32)
    freqs = ROPE_THETA ** (-jnp.arange(ROPE_PAIRS, dtype=_F32) / ROPE_PAIRS)
    ang = jnp.stack([row[:, None] * freqs, col[:, None] * freqs], axis=1)
    cos, sin = jnp.cos(ang), jnp.sin(ang)
    cos_h = jnp.stack([cos, cos], axis=2).reshape(t, HEAD_DIM)
    sin_h = jnp.stack([-sin, sin], axis=2).reshape(t, HEAD_DIM)
    return cos_h, sin_h


def _rope_tables(base, gains, scale):
    cos_h, sin_h = base
    g = gains.astype(_F32)
    g_sw = g.reshape(-1, 2, 2, ROPE_PAIRS)[:, :, ::-1, :].reshape(-1, HEAD_DIM)
    cos_t = cos_h[None] * (g * scale)[:, None, :]
    sin_t = sin_h[None] * (g_sw * scale)[:, None, :]
    reps = (1, 1, LANES // HEAD_DIM)
    return jnp.tile(cos_t, reps), jnp.tile(sin_t, reps)


def _trunk(x, mem, prep):
    for layer in range(DEPTH):
        u, gate, qt, qxt, k, vt = _proj(layer, x, prep["norm_pre"], prep["w_in"], prep["seg"],
                                        *prep["q_tables"], *prep["k_tables"])
        x = _mix(layer, qt, qxt, k, vt, mem, prep["mem_norm"], prep["w_mem_kv"], gate, u, x,
                 prep["w_pool"], prep["pool_scale"], prep["w_out"], prep["norm_post"])
    return x


def kernel(x_prompt, x_sample, mem_prompt, mem_sample, norm_pre, norm_post, w_in, pool_w, pool_scale,
           q_norm, k_norm, mem_norm, w_mem_kv, w_out):
    n_win = len(POOL_WINDOWS)
    eye = jnp.eye(n_win, dtype=pool_w.dtype)
    base = _rope_base(max(x_prompt.shape[1], x_sample.shape[1]))
    prep = {
        "norm_pre": norm_pre[:, None, :], "norm_post": norm_post[:, None, :],
        "mem_norm": mem_norm[:, None, :], "pool_scale": pool_scale[:, None, :],
        "w_in": w_in.astype(_BF16), "w_out": w_out.astype(_BF16), "w_mem_kv": w_mem_kv.astype(_BF16),
        "w_pool": jnp.einsum("gh,lgcd->lgchd", eye, pool_w).reshape(DEPTH, POOL_W, POOL_W).astype(_BF16),
        "seg": jnp.kron(jnp.eye(MXU_WIDTH // HEAD_DIM, dtype=_F32),
                        jnp.ones((HEAD_DIM, HEAD_DIM), _F32)).astype(_BF16),
        "q_tables": _rope_tables(base, q_norm, Q_SCALE),
        "k_tables": _rope_tables(base, k_norm, 1.0),
    }
    return (_trunk(x_prompt, mem_prompt, prep), _trunk(x_sample, mem_sample, prep))
```

```python
import math

import jax
import jax.numpy as jnp
from jax import lax
from jax.experimental import pallas as pl
from jax.experimental.pallas import tpu as pltpu

D_MODEL = 1024
DEPTH = 2
GRID_W = 64
N_MEM = 256
HEAD_DIM = 64
N_HEADS = 8
N_KV_HEADS = 2
ATTN_W = N_HEADS * HEAD_DIM
KV_W = N_KV_HEADS * HEAD_DIM
POOL_W = 256
POOL_WINDOWS = (2, 4, 8, 16)
POOL_GROUP = POOL_W // len(POOL_WINDOWS)
N_XHEADS = 4
XATTN_W = N_XHEADS * HEAD_DIM
MIX_W = POOL_W + ATTN_W + XATTN_W
IN_W = 2 * POOL_W + 2 * ATTN_W + 2 * KV_W + 2 * XATTN_W
ROPE_THETA = 10000.0
ROPE_PAIRS = HEAD_DIM // 4
EPS = 1e-6
Q_SCALE = math.log2(math.e) * HEAD_DIM ** -0.5

LANES = 128
MXU_WIDTH = 256
BF16_SUBLANES = 16
POOL_HALO = 8
PROJ_ROWS = 1024
PROJ_SUB = 512
KEY_CHUNK = 256
Q_ROWS = 512
Q_SUB = 256
PAIRS_PER_STREAM = 2
ATTN_LOOKAHEAD = 2
VMEM_LIMIT_BYTES = 56 * 1024 * 1024

VT_ROWS = HEAD_DIM + BF16_SUBLANES

_OFF_U = 0
_OFF_GP = POOL_W
_OFF_Q = 2 * POOL_W
_OFF_K = _OFF_Q + ATTN_W
_OFF_V = _OFF_K + KV_W
_OFF_GA = _OFF_V + KV_W
_OFF_QX = _OFF_GA + ATTN_W
_OFF_GX = _OFF_QX + XATTN_W

_BF16 = jnp.bfloat16
_F32 = jnp.float32


def _dot(a, b):
    return jnp.dot(a, b, preferred_element_type=_F32)


def _silu(g):
    return g * (1.0 / (1.0 + jnp.exp(-g)))


def _is_even_head_lane(shape):
    return (lax.broadcasted_iota(jnp.int32, shape, len(shape) - 1) % LANES) < HEAD_DIM


def _ones_row_tile(cols):
    row = lax.broadcasted_iota(jnp.int32, (BF16_SUBLANES, cols), 0)
    return jnp.where(row == 0, 1.0, 0.0).astype(_BF16)


def _memory_kv(mem_ref, g_ref, w_ref, k_ref, vt_ref):
    m = mem_ref[0]
    ms = jnp.mean(m * m, axis=-1, keepdims=True)
    mh = (m * lax.rsqrt(ms + EPS) * g_ref[...]).astype(_BF16)
    kv = _dot(mh, w_ref[...])
    k = kv[:, :XATTN_W]
    v_t = kv[:, XATTN_W:].T
    left = _is_even_head_lane((N_MEM, LANES))
    for pair in range(N_XHEADS // 2):
        kp = k[:, pair * LANES:(pair + 1) * LANES]
        k_ref[2 * pair] = jnp.where(left, kp, 0.0).astype(_BF16)
        k_ref[2 * pair + 1] = jnp.where(left, 0.0, kp).astype(_BF16)
    for head in range(N_XHEADS):
        vt_ref[head, 0:HEAD_DIM] = v_t[head * HEAD_DIM:(head + 1) * HEAD_DIM].astype(_BF16)
        vt_ref[head, HEAD_DIM:VT_ROWS] = _ones_row_tile(N_MEM)


def _swap16(x):
    n = x.shape[-1]
    lane = lax.broadcasted_iota(jnp.int32, x.shape, 1)
    first = (lane & ROPE_PAIRS) == 0
    return jnp.where(first, pltpu.roll(x, n - ROPE_PAIRS, 1), pltpu.roll(x, ROPE_PAIRS, 1))


def _norm_rope(z, seg, cos_tab, sin_tab):
    reps = z.shape[1] // LANES
    cos_tab = jnp.concatenate([cos_tab] * reps, axis=1)
    sin_tab = jnp.concatenate([sin_tab] * reps, axis=1)
    ssq = _dot((z * z).astype(_BF16), seg)
    r = lax.rsqrt(ssq * (1.0 / HEAD_DIM) + EPS)
    return r * (z * cos_tab + _swap16(z) * sin_tab)


def _proj_kernel(x_ref, g_ref, w_ref, seg_ref, cq_ref, sq_ref, ck_ref, sk_ref,
                 u_ref, gate_ref, qt_ref, qxt_ref, k_ref, vt_ref):
    seg = seg_ref[...]
    seg_w = seg.shape[0]
    for r0 in range(0, x_ref.shape[1], PROJ_SUB):
        rs = slice(r0, r0 + PROJ_SUB)
        x = x_ref[0, rs, :]
        ms = jnp.mean(x * x, axis=-1, keepdims=True)
        h = (x * lax.rsqrt(ms + EPS) * g_ref[...]).astype(_BF16)

        z_qkv = _dot(h, w_ref[:, _OFF_Q:_OFF_GA])
        z_pool = _dot(h, w_ref[:, _OFF_U:_OFF_Q])

        cq, sq = cq_ref[rs, :], sq_ref[rs, :]
        for lo in range(0, ATTN_W, seg_w):
            q = _norm_rope(z_qkv[:, lo:lo + seg_w], seg, cq, sq)
            for pair in range(lo // LANES, (lo + seg_w) // LANES):
                qt_ref[0, pair, :, rs] = q[:, pair * LANES - lo:(pair + 1) * LANES - lo].T.astype(_BF16)

        k = _norm_rope(z_qkv[:, ATTN_W:ATTN_W + KV_W], seg[:KV_W, :KV_W],
                       ck_ref[rs, :], sk_ref[rs, :])
        k_sw = pltpu.roll(k, HEAD_DIM, 1)
        left = _is_even_head_lane(k.shape)
        k_ref[0, 0, rs, :] = jnp.where(left, k, 0.0).astype(_BF16)
        k_ref[0, 1, rs, :] = jnp.where(left, 0.0, k_sw).astype(_BF16)
        k_ref[0, 2, rs, :] = jnp.where(left, k_sw, 0.0).astype(_BF16)
        k_ref[0, 3, rs, :] = jnp.where(left, 0.0, k).astype(_BF16)

        v_t = z_qkv[:, ATTN_W + KV_W:].T
        for head in range(N_KV_HEADS):
            vt_ref[0, head, 0:HEAD_DIM, rs] = v_t[head * HEAD_DIM:(head + 1) * HEAD_DIM].astype(_BF16)
            vt_ref[0, head, HEAD_DIM:VT_ROWS, rs] = _ones_row_tile(PROJ_SUB)

        u_ref[0, rs, :] = z_pool[:, 0:POOL_W]
        gate_ref[0, rs, 0:POOL_W] = _silu(z_pool[:, POOL_W:]).astype(_BF16)

        z_rest = _dot(h, w_ref[:, _OFF_GA:])
        gate_ref[0, rs, POOL_W:POOL_W + ATTN_W] = _silu(z_rest[:, 0:ATTN_W]).astype(_BF16)
        gate_ref[0, rs, POOL_W + ATTN_W:MIX_W] = _silu(z_rest[:, ATTN_W + XATTN_W:]).astype(_BF16)
        for pair in range(N_XHEADS // 2):
            lo = ATTN_W + pair * LANES
            qxt_ref[0, pair, :, rs] = (z_rest[:, lo:lo + LANES] * Q_SCALE).T.astype(_BF16)


def _proj(layer, x, g, w, seg, cq, sq, ck, sk):
    b, t, _ = x.shape
    rows = PROJ_ROWS
    n = t // rows
    tab = pl.BlockSpec((None, rows, LANES), lambda i, j: (layer, j, 0))
    return pl.pallas_call(
        _proj_kernel,
        grid=(b, n),
        in_specs=[
            pl.BlockSpec((1, rows, D_MODEL), lambda i, j: (i, j, 0)),
            pl.BlockSpec((None, 1, D_MODEL), lambda i, j: (layer, 0, 0)),
            pl.BlockSpec((None, D_MODEL, IN_W), lambda i, j: (layer, 0, 0)),
            pl.BlockSpec((MXU_WIDTH, MXU_WIDTH), lambda i, j: (0, 0)),
            tab, tab, tab, tab,
        ],
        out_specs=[
            pl.BlockSpec((1, rows, POOL_W), lambda i, j: (i, j, 0)),
            pl.BlockSpec((1, rows, MIX_W), lambda i, j: (i, j, 0)),
            pl.BlockSpec((1, N_HEADS // 2, LANES, rows), lambda i, j: (i, 0, 0, j)),
            pl.BlockSpec((1, N_XHEADS // 2, LANES, rows), lambda i, j: (i, 0, 0, j)),
            pl.BlockSpec((1, 2 * N_KV_HEADS, rows, KV_W), lambda i, j: (i, 0, j, 0)),
            pl.BlockSpec((1, N_KV_HEADS, VT_ROWS, rows), lambda i, j: (i, 0, 0, j)),
        ],
        out_shape=[
            jax.ShapeDtypeStruct((b, t, POOL_W), _F32),
            jax.ShapeDtypeStruct((b, t, MIX_W), _BF16),
            jax.ShapeDtypeStruct((b, N_HEADS // 2, LANES, t), _BF16),
            jax.ShapeDtypeStruct((b, N_XHEADS // 2, LANES, t), _BF16),
            jax.ShapeDtypeStruct((b, 2 * N_KV_HEADS, t, KV_W), _BF16),
            jax.ShapeDtypeStruct((b, N_KV_HEADS, VT_ROWS, t), _BF16),
        ],
        compiler_params=pltpu.CompilerParams(
            dimension_semantics=("arbitrary", "arbitrary"), vmem_limit_bytes=VMEM_LIMIT_BYTES),
        name="proj",
    )(x, g, w, seg, cq, sq, ck, sk)


def _attend(streams, lookahead):
    n_chunks = [s[1].shape[1] // s[5] for s in streams]
    units = [(c, i, parity) for c in range(max(n_chunks)) for i in range(len(streams))
             if c < n_chunks[i] for parity in range(2)]
    units = [u for u in units if n_chunks[u[1]] > 1] + [u for u in units if n_chunks[u[1]] == 1]
    state = [[None, None] for _ in streams]
    scores = {}

    def issue_scores(c, i, parity):
        q_t, k_ref, k_slot = streams[i][:3]
        chunk = streams[i][5]
        return _dot(k_ref[k_slot + parity, c * chunk:(c + 1) * chunk, :], q_t)

    def finish(c, i, parity, s):
        _, _, _, vt_ref, v_slots, chunk, sink = streams[i]
        mn = jnp.max(s, axis=0, keepdims=True)
        if state[i][parity] is not None:
            m, acc = state[i][parity]
            mn = jnp.maximum(m, mn)
        p = jnp.exp2((s - mn).astype(_BF16))
        pv = _dot(vt_ref[v_slots[parity], :, c * chunk:(c + 1) * chunk], p)
        if state[i][parity] is not None:
            pv = acc * jnp.exp2(m - mn) + pv
        state[i][parity] = (mn, pv)
        if c == n_chunks[i] - 1 and parity == 1:
            out_t = [acc[0:HEAD_DIM] * (1.0 / acc[HEAD_DIM:HEAD_DIM + 1]) for _, acc in state[i]]
            sink(jnp.concatenate(out_t, axis=0).T)

    for step in range(len(units) + lookahead):
        if step < len(units):
            scores[step] = issue_scores(*units[step])
        if step >= lookahead:
            finish(*units[step - lookahead], scores.pop(step - lookahead))


def _pool_delta(u_ref, up_ref, un_ref, inv_ref):
    tq = u_ref.shape[1]
    n_ext = tq + 2 * POOL_HALO
    is_first = pl.program_id(1) == 0
    is_last = pl.program_id(1) == pl.num_programs(1) - 1
    left = _is_even_head_lane((tq, LANES))
    deltas = []
    for half in range(POOL_W // LANES):
        w_a, w_b = POOL_WINDOWS[2 * half], POOL_WINDOWS[2 * half + 1]
        cols = slice(half * LANES, (half + 1) * LANES)
        u = u_ref[0, :, cols]
        ext = jnp.concatenate([jnp.where(is_first, 0.0, up_ref[0, :, cols]), u,
                               jnp.where(is_last, 0.0, un_ref[0, :, cols])], axis=0)
        sums = {}
        run = ext
        for w in POOL_WINDOWS[:2 * half + 2]:
            run = run + pltpu.roll(run, n_ext - w // 2, 0)
            if w in (w_a, w_b):
                aligned = run if w // 2 == POOL_HALO else pltpu.roll(run, w // 2, 0)
                first = 0 if w // 2 == POOL_HALO else POOL_HALO
                sums[w] = aligned[first:first + tq]
        mean = jnp.where(left, sums[w_a], sums[w_b]) * inv_ref[:, cols]
        deltas.append(mean - u)
    return jnp.concatenate(deltas, axis=1)


def _mix_kernel(qt_ref, qxt_ref, k_ref, vt_ref, mem_ref, gmem_ref, wmem_ref, gate_ref, u_ref, up_ref,
                un_ref, inv_ref, x_ref, wpool_ref, pscale_ref, wout_ref, gpost_ref, y_ref,
                mix_ref, km_ref, vmt_ref):
    @pl.when(pl.program_id(1) == 0)
    def _():
        _memory_kv(mem_ref, gmem_ref, wmem_ref, km_ref, vmt_ref)

    d = _pool_delta(u_ref, up_ref, un_ref, inv_ref).astype(_BF16)
    pool = _dot(d, wpool_ref[...]) * pscale_ref[...]
    mix_ref[:, 0:POOL_W] = (pool * gate_ref[0, :, 0:POOL_W].astype(_F32)).astype(_BF16)

    tq = qt_ref.shape[3]
    pairs_per_group = N_HEADS // N_KV_HEADS // 2

    def gated_sink(r0, first_cols):
        def sink(out):
            for n, c0 in enumerate(first_cols):
                gate = gate_ref[0, r0:r0 + Q_SUB, c0:c0 + LANES].astype(_F32)
                mix_ref[r0:r0 + Q_SUB, c0:c0 + LANES] = (out[n * Q_SUB:(n + 1) * Q_SUB] * gate).astype(_BF16)
        return sink

    streams = []
    for r0 in range(0, tq, Q_SUB):
        cols = slice(r0, r0 + Q_SUB)
        for pair in range(N_XHEADS // 2):
            streams.append((qxt_ref[0, pair, :, cols], km_ref, 2 * pair,
                            vmt_ref, (2 * pair, 2 * pair + 1), N_MEM,
                            gated_sink(r0, [POOL_W + ATTN_W + pair * LANES])))
        for first in range(0, N_HEADS // 2, PAIRS_PER_STREAM):
            group = first // pairs_per_group
            pairs = list(range(first, first + PAIRS_PER_STREAM))
            q_t = jnp.concatenate([qt_ref[0, pair, :, cols] for pair in pairs], axis=1)
            streams.append((q_t, k_ref.at[0], 2 * group, vt_ref.at[0], (group, group), KEY_CHUNK,
                            gated_sink(r0, [POOL_W + pair * LANES for pair in pairs])))

    _attend(streams, ATTN_LOOKAHEAD)

    y = _dot(mix_ref[...], wout_ref[...])
    ms = jnp.mean(y * y, axis=-1, keepdims=True)
    y_ref[0] = x_ref[0] + y * lax.rsqrt(ms + EPS) * gpost_ref[...]


def _mix(layer, qt, qxt, k, vt, mem, gmem, wmem, gate, u, pool_inv, x, wpool, pscale, wout, gpost):
    b, t, _ = x.shape
    tq = Q_ROWS
    n = t // tq
    halo_per_q = tq // POOL_HALO
    n_halo = t // POOL_HALO
    return pl.pallas_call(
        _mix_kernel,
        grid=(b, n),
        in_specs=[
            pl.BlockSpec((1, N_HEADS // 2, LANES, tq), lambda i, j: (i, 0, 0, j)),
            pl.BlockSpec((1, N_XHEADS // 2, LANES, tq), lambda i, j: (i, 0, 0, j)),
            pl.BlockSpec((1, 2 * N_KV_HEADS, t, KV_W), lambda i, j: (i, 0, 0, 0)),
            pl.BlockSpec((1, N_KV_HEADS, VT_ROWS, t), lambda i, j: (i, 0, 0, 0)),
            pl.BlockSpec((1, N_MEM, D_MODEL), lambda i, j: (i, 0, 0)),
            pl.BlockSpec((None, 1, D_MODEL), lambda i, j: (layer, 0, 0)),
            pl.BlockSpec((None, D_MODEL, 2 * XATTN_W), lambda i, j: (layer, 0, 0)),
            pl.BlockSpec((1, tq, MIX_W), lambda i, j: (i, j, 0)),
            pl.BlockSpec((1, tq, POOL_W), lambda i, j: (i, j, 0)),
            pl.BlockSpec((1, POOL_HALO, POOL_W),
                         lambda i, j: (i, jnp.maximum(j * halo_per_q - 1, 0), 0)),
            pl.BlockSpec((1, POOL_HALO, POOL_W),
                         lambda i, j: (i, jnp.minimum((j + 1) * halo_per_q, n_halo - 1), 0)),
            pl.BlockSpec((tq, POOL_W), lambda i, j: (j, 0)),
            pl.BlockSpec((1, tq, D_MODEL), lambda i, j: (i, j, 0)),
            pl.BlockSpec((None, POOL_W, POOL_W), lambda i, j: (layer, 0, 0)),
            pl.BlockSpec((None, 1, POOL_W), lambda i, j: (layer, 0, 0)),
            pl.BlockSpec((None, MIX_W, D_MODEL), lambda i, j: (layer, 0, 0)),
            pl.BlockSpec((None, 1, D_MODEL), lambda i, j: (layer, 0, 0)),
        ],
        out_specs=pl.BlockSpec((1, tq, D_MODEL), lambda i, j: (i, j, 0)),
        out_shape=jax.ShapeDtypeStruct((b, t, D_MODEL), _F32),
        scratch_shapes=[
            pltpu.VMEM((tq, MIX_W), _BF16),
            pltpu.VMEM((N_XHEADS, N_MEM, LANES), _BF16),
            pltpu.VMEM((N_XHEADS, VT_ROWS, N_MEM), _BF16),
        ],
        compiler_params=pltpu.CompilerParams(
            dimension_semantics=("arbitrary", "arbitrary"), vmem_limit_bytes=VMEM_LIMIT_BYTES),
        name="mix",
    )(qt, qxt, k, vt, mem, gmem, wmem, gate, u, u, u, pool_inv, x, wpool, pscale, wout, gpost)


def _rope_base(t):
    rows = t // GRID_W
    row = jnp.repeat(jnp.arange(rows), GRID_W).astype(_F32)
    col = jnp.tile(jnp.arange(GRID_W), rows).astype(_F32)
    freqs = ROPE_THETA ** (-jnp.arange(ROPE_PAIRS, dtype=_F32) / ROPE_PAIRS)
    ang = jnp.stack([row[:, None] * freqs, col[:, None] * freqs], axis=1)
    cos, sin = jnp.cos(ang), jnp.sin(ang)
    cos_h = jnp.stack([cos, cos], axis=2).reshape(t, HEAD_DIM)
    sin_h = jnp.stack([-sin, sin], axis=2).reshape(t, HEAD_DIM)
    return cos_h, sin_h


def _rope_tables(base, gains, scale):
    cos_h, sin_h = base
    g = gains.astype(_F32)
    g_sw = g.reshape(-1, 2, 2, ROPE_PAIRS)[:, :, ::-1, :].reshape(-1, HEAD_DIM)
    cos_t = cos_h[None] * (g * scale)[:, None, :]
    sin_t = sin_h[None] * (g_sw * scale)[:, None, :]
    reps = (1, 1, LANES // HEAD_DIM)
    return jnp.tile(cos_t, reps), jnp.tile(sin_t, reps)


def _pool_inv_counts(t):
    pos = jnp.arange(t)[:, None]
    w = jnp.repeat(jnp.array(POOL_WINDOWS), POOL_GROUP)[None, :]
    cnt = jnp.minimum(pos + (w - w // 2), t) - jnp.maximum(pos - w // 2, 0)
    return 1.0 / cnt.astype(_F32)


def _trunk(x, mem, prep):
    pool_inv = _pool_inv_counts(x.shape[1])
    for layer in range(DEPTH):
        u, gate, qt, qxt, k, vt = _proj(layer, x, prep["norm_pre"], prep["w_in"], prep["seg"],
                                        *prep["q_tables"], *prep["k_tables"])
        x = _mix(layer, qt, qxt, k, vt, mem, prep["mem_norm"], prep["w_mem_kv"], gate, u, pool_inv, x,
                 prep["w_pool"], prep["pool_scale"], prep["w_out"], prep["norm_post"])
    return x


def kernel(x_prompt, x_sample, mem_prompt, mem_sample, norm_pre, norm_post, w_in, pool_w, pool_scale,
           q_norm, k_norm, mem_norm, w_mem_kv, w_out):
    n_win = len(POOL_WINDOWS)
    eye = jnp.eye(n_win, dtype=pool_w.dtype)
    base = _rope_base(max(x_prompt.shape[1], x_sample.shape[1]))
    prep = {
        "norm_pre": norm_pre[:, None, :], "norm_post": norm_post[:, None, :],
        "mem_norm": mem_norm[:, None, :], "pool_scale": pool_scale[:, None, :],
        "w_in": w_in.astype(_BF16), "w_out": w_out.astype(_BF16), "w_mem_kv": w_mem_kv.astype(_BF16),
        "w_pool": jnp.einsum("gh,lgcd->lgchd", eye, pool_w).reshape(DEPTH, POOL_W, POOL_W).astype(_BF16),
        "seg": jnp.kron(jnp.eye(MXU_WIDTH // HEAD_DIM, dtype=_F32),
                        jnp.ones((HEAD_DIM, HEAD_DIM), _F32)).astype(_BF16),
        "q_tables": _rope_tables(base, q_norm, Q_SCALE),
        "k_tables": _rope_tables(base, k_norm, 1.0),
    }
    return (_trunk(x_prompt, mem_prompt, prep), _trunk(x_sample, mem_sample, prep))
```

```python
import math

import jax
import jax.numpy as jnp
from jax import lax
from jax.experimental import pallas as pl
from jax.experimental.pallas import tpu as pltpu

D_MODEL = 1024
DEPTH = 2
GRID_W = 64
N_MEM = 256
HEAD_DIM = 64
N_HEADS = 8
N_KV_HEADS = 2
ATTN_W = N_HEADS * HEAD_DIM
KV_W = N_KV_HEADS * HEAD_DIM
POOL_W = 256
POOL_WINDOWS = (2, 4, 8, 16)
POOL_GROUP = POOL_W // len(POOL_WINDOWS)
N_XHEADS = 4
XATTN_W = N_XHEADS * HEAD_DIM
MIX_W = POOL_W + ATTN_W + XATTN_W
IN_W = 2 * POOL_W + 2 * ATTN_W + 2 * KV_W + 2 * XATTN_W
ROPE_THETA = 10000.0
ROPE_PAIRS = HEAD_DIM // 4
EPS = 1e-6
Q_SCALE = math.log2(math.e) * HEAD_DIM ** -0.5

LANES = 128
MXU_WIDTH = 256
BF16_SUBLANES = 16
POOL_HALO = 8
PROJ_ROWS = 1024
PROJ_SUB = 512
KEY_CHUNK = 256
Q_ROWS = 512
Q_SUB = 256
PAIRS_PER_STREAM = 2
ATTN_LOOKAHEAD = 2
VMEM_LIMIT_BYTES = 56 * 1024 * 1024

VT_ROWS = HEAD_DIM + BF16_SUBLANES

_OFF_U = 0
_OFF_GP = POOL_W
_OFF_Q = 2 * POOL_W
_OFF_K = _OFF_Q + ATTN_W
_OFF_V = _OFF_K + KV_W
_OFF_GA = _OFF_V + KV_W
_OFF_QX = _OFF_GA + ATTN_W
_OFF_GX = _OFF_QX + XATTN_W

_BF16 = jnp.bfloat16
_F32 = jnp.float32


def _dot(a, b):
    return jnp.dot(a, b, preferred_element_type=_F32)


def _silu(g):
    half = 0.5 * g
    return half + half * jnp.tanh(half)


def _is_even_head_lane(shape):
    return (lax.broadcasted_iota(jnp.int32, shape, len(shape) - 1) % LANES) < HEAD_DIM


def _ones_row_tile(cols):
    row = lax.broadcasted_iota(jnp.int32, (BF16_SUBLANES, cols), 0)
    return jnp.where(row == 0, 1.0, 0.0).astype(_BF16)


def _memory_kv(mem_ref, g_ref, w_ref, k_ref, vt_ref):
    m = mem_ref[0]
    ms = jnp.mean(m * m, axis=-1, keepdims=True)
    mh = (m * lax.rsqrt(ms + EPS) * g_ref[...]).astype(_BF16)
    kv = _dot(mh, w_ref[...])
    k = kv[:, :XATTN_W]
    v_t = kv[:, XATTN_W:].T
    left = _is_even_head_lane((N_MEM, LANES))
    for pair in range(N_XHEADS // 2):
        kp = k[:, pair * LANES:(pair + 1) * LANES]
        k_ref[2 * pair] = jnp.where(left, kp, 0.0).astype(_BF16)
        k_ref[2 * pair + 1] = jnp.where(left, 0.0, kp).astype(_BF16)
    for head in range(N_XHEADS):
        vt_ref[head, 0:HEAD_DIM] = v_t[head * HEAD_DIM:(head + 1) * HEAD_DIM].astype(_BF16)
        vt_ref[head, HEAD_DIM:VT_ROWS] = _ones_row_tile(N_MEM)


def _swap16(x):
    n = x.shape[-1]
    lane = lax.broadcasted_iota(jnp.int32, x.shape, 1)
    first = (lane & ROPE_PAIRS) == 0
    return jnp.where(first, pltpu.roll(x, n - ROPE_PAIRS, 1), pltpu.roll(x, ROPE_PAIRS, 1))


def _norm_rope(z, seg, cos_tab, sin_tab):
    reps = z.shape[1] // LANES
    cos_tab = jnp.concatenate([cos_tab] * reps, axis=1)
    sin_tab = jnp.concatenate([sin_tab] * reps, axis=1)
    ssq = _dot((z * z).astype(_BF16), seg)
    r = lax.rsqrt(ssq * (1.0 / HEAD_DIM) + EPS)
    return r * (z * cos_tab + _swap16(z) * sin_tab)


def _proj_kernel(x_ref, g_ref, w_ref, seg_ref, cq_ref, sq_ref, ck_ref, sk_ref,
                 u_ref, gate_ref, qt_ref, qxt_ref, k_ref, vt_ref):
    seg = seg_ref[...]
    seg_w = seg.shape[0]
    for r0 in range(0, x_ref.shape[1], PROJ_SUB):
        rs = slice(r0, r0 + PROJ_SUB)
        x = x_ref[0, rs, :]
        ms = jnp.mean(x * x, axis=-1, keepdims=True)
        h = (x * lax.rsqrt(ms + EPS) * g_ref[...]).astype(_BF16)

        z_qkv = _dot(h, w_ref[:, _OFF_Q:_OFF_GA])
        z_pool = _dot(h, w_ref[:, _OFF_U:_OFF_Q])

        cq, sq = cq_ref[rs, :], sq_ref[rs, :]
        for lo in range(0, ATTN_W, seg_w):
            q = _norm_rope(z_qkv[:, lo:lo + seg_w], seg, cq, sq)
            for pair in range(lo // LANES, (lo + seg_w) // LANES):
                qt_ref[0, pair, :, rs] = q[:, pair * LANES - lo:(pair + 1) * LANES - lo].T.astype(_BF16)

        k = _norm_rope(z_qkv[:, ATTN_W:ATTN_W + KV_W], seg[:KV_W, :KV_W],
                       ck_ref[rs, :], sk_ref[rs, :])
        k_sw = pltpu.roll(k, HEAD_DIM, 1)
        left = _is_even_head_lane(k.shape)
        k_ref[0, 0, rs, :] = jnp.where(left, k, 0.0).astype(_BF16)
        k_ref[0, 1, rs, :] = jnp.where(left, 0.0, k_sw).astype(_BF16)
        k_ref[0, 2, rs, :] = jnp.where(left, k_sw, 0.0).astype(_BF16)
        k_ref[0, 3, rs, :] = jnp.where(left, 0.0, k).astype(_BF16)

        v_t = z_qkv[:, ATTN_W + KV_W:].T
        for head in range(N_KV_HEADS):
            vt_ref[0, head, 0:HEAD_DIM, rs] = v_t[head * HEAD_DIM:(head + 1) * HEAD_DIM].astype(_BF16)
            vt_ref[0, head, HEAD_DIM:VT_ROWS, rs] = _ones_row_tile(PROJ_SUB)

        u_ref[0, rs, :] = z_pool[:, 0:POOL_W]
        gate_ref[0, rs, 0:POOL_W] = _silu(z_pool[:, POOL_W:]).astype(_BF16)

        z_rest = _dot(h, w_ref[:, _OFF_GA:])
        gate_ref[0, rs, POOL_W:POOL_W + ATTN_W] = _silu(z_rest[:, 0:ATTN_W]).astype(_BF16)
        gate_ref[0, rs, POOL_W + ATTN_W:MIX_W] = _silu(z_rest[:, ATTN_W + XATTN_W:]).astype(_BF16)
        for pair in range(N_XHEADS // 2):
            lo = ATTN_W + pair * LANES
            qxt_ref[0, pair, :, rs] = (z_rest[:, lo:lo + LANES] * Q_SCALE).T.astype(_BF16)


def _proj(layer, x, g, w, seg, cq, sq, ck, sk):
    b, t, _ = x.shape
    rows = PROJ_ROWS
    assert t % rows == 0 and rows % PROJ_SUB == 0 and t % GRID_W == 0, (t, rows)
    n = t // rows
    tab = pl.BlockSpec((None, rows, LANES), lambda i, j: (layer, j, 0))
    return pl.pallas_call(
        _proj_kernel,
        grid=(b, n),
        in_specs=[
            pl.BlockSpec((1, rows, D_MODEL), lambda i, j: (i, j, 0)),
            pl.BlockSpec((None, 1, D_MODEL), lambda i, j: (layer, 0, 0)),
            pl.BlockSpec((None, D_MODEL, IN_W), lambda i, j: (layer, 0, 0)),
            pl.BlockSpec((MXU_WIDTH, MXU_WIDTH), lambda i, j: (0, 0)),
            tab, tab, tab, tab,
        ],
        out_specs=[
            pl.BlockSpec((1, rows, POOL_W), lambda i, j: (i, j, 0)),
            pl.BlockSpec((1, rows, MIX_W), lambda i, j: (i, j, 0)),
            pl.BlockSpec((1, N_HEADS // 2, LANES, rows), lambda i, j: (i, 0, 0, j)),
            pl.BlockSpec((1, N_XHEADS // 2, LANES, rows), lambda i, j: (i, 0, 0, j)),
            pl.BlockSpec((1, 2 * N_KV_HEADS, rows, KV_W), lambda i, j: (i, 0, j, 0)),
            pl.BlockSpec((1, N_KV_HEADS, VT_ROWS, rows), lambda i, j: (i, 0, 0, j)),
        ],
        out_shape=[
            jax.ShapeDtypeStruct((b, t, POOL_W), _F32),
            jax.ShapeDtypeStruct((b, t, MIX_W), _BF16),
            jax.ShapeDtypeStruct((b, N_HEADS // 2, LANES, t), _BF16),
            jax.ShapeDtypeStruct((b, N_XHEADS // 2, LANES, t), _BF16),
            jax.ShapeDtypeStruct((b, 2 * N_KV_HEADS, t, KV_W), _BF16),
            jax.ShapeDtypeStruct((b, N_KV_HEADS, VT_ROWS, t), _BF16),
        ],
        compiler_params=pltpu.CompilerParams(
            dimension_semantics=("arbitrary", "arbitrary"), vmem_limit_bytes=VMEM_LIMIT_BYTES),
        name="proj",
    )(x, g, w, seg, cq, sq, ck, sk)


def _attend(streams, lookahead):
    n_chunks = [s[1].shape[1] // s[5] for s in streams]
    units = [(c, i, parity) for c in range(max(n_chunks)) for i in range(len(streams))
             if c < n_chunks[i] for parity in range(2)]
    multi = [u for u in units if n_chunks[u[1]] > 1]
    single = [u for u in units if n_chunks[u[1]] == 1]
    order = []
    for n, unit in enumerate(multi):
        order.append(("scores", unit))
        if n >= lookahead:
            order.append(("finish", multi[n - lookahead]))
    order += [("scores", unit) for unit in single]
    order += [("finish", unit) for unit in multi[max(len(multi) - lookahead, 0):] + single]
    state = [[None, None] for _ in streams]
    scores = {}

    def issue_scores(c, i, parity):
        q_t, k_ref, k_slot = streams[i][:3]
        chunk = streams[i][5]
        return _dot(k_ref[k_slot + parity, c * chunk:(c + 1) * chunk, :], q_t)

    def finish(c, i, parity, s):
        _, _, _, vt_ref, v_slots, chunk, sink = streams[i]
        mn = jnp.max(s, axis=0, keepdims=True)
        if state[i][parity] is not None:
            m, acc = state[i][parity]
            mn = jnp.maximum(m, mn)
        p = jnp.exp2((s - mn).astype(_BF16))
        pv = _dot(vt_ref[v_slots[parity], :, c * chunk:(c + 1) * chunk], p)
        if state[i][parity] is not None:
            pv = acc * jnp.exp2(m - mn) + pv
        state[i][parity] = (mn, pv)
        if c == n_chunks[i] - 1 and parity == 1:
            out_t = [acc[0:HEAD_DIM] * (1.0 / acc[HEAD_DIM:HEAD_DIM + 1]) for _, acc in state[i]]
            sink(jnp.concatenate(out_t, axis=0).T)

    for what, unit in order:
        if what == "scores":
            scores[unit] = issue_scores(*unit)
        else:
            finish(*unit, scores.pop(unit))


def _pool_delta(u_ref, up_ref, un_ref, inv_ref):
    tq = u_ref.shape[1]
    n_ext = tq + 2 * POOL_HALO
    is_first = pl.program_id(1) == 0
    is_last = pl.program_id(1) == pl.num_programs(1) - 1
    left = _is_even_head_lane((tq, LANES))
    deltas = []
    for half in range(POOL_W // LANES):
        w_a, w_b = POOL_WINDOWS[2 * half], POOL_WINDOWS[2 * half + 1]
        cols = slice(half * LANES, (half + 1) * LANES)
        u = u_ref[0, :, cols]
        ext = jnp.concatenate([jnp.where(is_first, 0.0, up_ref[0, :, cols]), u,
                               jnp.where(is_last, 0.0, un_ref[0, :, cols])], axis=0)
        sums = {}
        run = ext
        for w in POOL_WINDOWS[:2 * half + 2]:
            run = run + pltpu.roll(run, n_ext - w // 2, 0)
            if w in (w_a, w_b):
                aligned = run if w // 2 == POOL_HALO else pltpu.roll(run, w // 2, 0)
                first = 0 if w // 2 == POOL_HALO else POOL_HALO
                sums[w] = aligned[first:first + tq]
        mean = jnp.where(left, sums[w_a], sums[w_b]) * inv_ref[:, cols]
        deltas.append(mean - u)
    return jnp.concatenate(deltas, axis=1)


def _mix_kernel(qt_ref, qxt_ref, k_ref, vt_ref, mem_ref, gmem_ref, wmem_ref, gate_ref, u_ref, up_ref,
                un_ref, inv_ref, x_ref, wpool_ref, pscale_ref, wout_ref, gpost_ref, y_ref,
                mix_ref, km_ref, vmt_ref):
    @pl.when(pl.program_id(1) == 0)
    def _():
        _memory_kv(mem_ref, gmem_ref, wmem_ref, km_ref, vmt_ref)

    d = _pool_delta(u_ref, up_ref, un_ref, inv_ref).astype(_BF16)
    pool = _dot(d, wpool_ref[...]) * pscale_ref[...]
    mix_ref[:, 0:POOL_W] = (pool * gate_ref[0, :, 0:POOL_W].astype(_F32)).astype(_BF16)

    tq = qt_ref.shape[3]
    pairs_per_group = N_HEADS // N_KV_HEADS // 2

    def gated_sink(r0, first_cols):
        def sink(out):
            for n, c0 in enumerate(first_cols):
                gate = gate_ref[0, r0:r0 + Q_SUB, c0:c0 + LANES].astype(_F32)
                mix_ref[r0:r0 + Q_SUB, c0:c0 + LANES] = (out[n * Q_SUB:(n + 1) * Q_SUB] * gate).astype(_BF16)
        return sink

    streams = []
    for r0 in range(0, tq, Q_SUB):
        cols = slice(r0, r0 + Q_SUB)
        for pair in range(N_XHEADS // 2):
            streams.append((qxt_ref[0, pair, :, cols], km_ref, 2 * pair,
                            vmt_ref, (2 * pair, 2 * pair + 1), N_MEM,
                            gated_sink(r0, [POOL_W + ATTN_W + pair * LANES])))
        for first in range(0, N_HEADS // 2, PAIRS_PER_STREAM):
            group = first // pairs_per_group
            pairs = list(range(first, first + PAIRS_PER_STREAM))
            q_t = jnp.concatenate([qt_ref[0, pair, :, cols] for pair in pairs], axis=1)
            streams.append((q_t, k_ref.at[0], 2 * group, vt_ref.at[0], (group, group), KEY_CHUNK,
                            gated_sink(r0, [POOL_W + pair * LANES for pair in pairs])))

    _attend(streams, ATTN_LOOKAHEAD)

    y = _dot(mix_ref[...], wout_ref[...])
    ms = jnp.mean(y * y, axis=-1, keepdims=True)
    y_ref[0] = x_ref[0] + y * lax.rsqrt(ms + EPS) * gpost_ref[...]


def _mix(layer, qt, qxt, k, vt, mem, gmem, wmem, gate, u, pool_inv, x, wpool, pscale, wout, gpost):
    b, t, _ = x.shape
    tq = Q_ROWS
    assert t % tq == 0 and tq % Q_SUB == 0 and t % KEY_CHUNK == 0, (t, tq)
    n = t // tq
    halo_per_q = tq // POOL_HALO
    n_halo = t // POOL_HALO
    return pl.pallas_call(
        _mix_kernel,
        grid=(b, n),
        in_specs=[
            pl.BlockSpec((1, N_HEADS // 2, LANES, tq), lambda i, j: (i, 0, 0, j)),
            pl.BlockSpec((1, N_XHEADS // 2, LANES, tq), lambda i, j: (i, 0, 0, j)),
            pl.BlockSpec((1, 2 * N_KV_HEADS, t, KV_W), lambda i, j: (i, 0, 0, 0)),
            pl.BlockSpec((1, N_KV_HEADS, VT_ROWS, t), lambda i, j: (i, 0, 0, 0)),
            pl.BlockSpec((1, N_MEM, D_MODEL), lambda i, j: (i, 0, 0)),
            pl.BlockSpec((None, 1, D_MODEL), lambda i, j: (layer, 0, 0)),
            pl.BlockSpec((None, D_MODEL, 2 * XATTN_W), lambda i, j: (layer, 0, 0)),
            pl.BlockSpec((1, tq, MIX_W), lambda i, j: (i, j, 0)),
            pl.BlockSpec((1, tq, POOL_W), lambda i, j: (i, j, 0)),
            pl.BlockSpec((1, POOL_HALO, POOL_W),
                         lambda i, j: (i, jnp.maximum(j * halo_per_q - 1, 0), 0)),
            pl.BlockSpec((1, POOL_HALO, POOL_W),
                         lambda i, j: (i, jnp.minimum((j + 1) * halo_per_q, n_halo - 1), 0)),
            pl.BlockSpec((tq, POOL_W), lambda i, j: (j, 0)),
            pl.BlockSpec((1, tq, D_MODEL), lambda i, j: (i, j, 0)),
            pl.BlockSpec((None, POOL_W, POOL_W), lambda i, j: (layer, 0, 0)),
            pl.BlockSpec((None, 1, POOL_W), lambda i, j: (layer, 0, 0)),
            pl.BlockSpec((None, MIX_W, D_MODEL), lambda i, j: (layer, 0, 0)),
            pl.BlockSpec((None, 1, D_MODEL), lambda i, j: (layer, 0, 0)),
        ],
        out_specs=pl.BlockSpec((1, tq, D_MODEL), lambda i, j: (i, j, 0)),
        out_shape=jax.ShapeDtypeStruct((b, t, D_MODEL), _F32),
        scratch_shapes=[
            pltpu.VMEM((tq, MIX_W), _BF16),
            pltpu.VMEM((N_XHEADS, N_MEM, LANES), _BF16),
            pltpu.VMEM((N_XHEADS, VT_ROWS, N_MEM), _BF16),
        ],
        compiler_params=pltpu.CompilerParams(
            dimension_semantics=("arbitrary", "arbitrary"), vmem_limit_bytes=VMEM_LIMIT_BYTES),
        name="mix",
    )(qt, qxt, k, vt, mem, gmem, wmem, gate, u, u, u, pool_inv, x, wpool, pscale, wout, gpost)


def _rope_base(t):
    rows = t // GRID_W
    row = jnp.repeat(jnp.arange(rows), GRID_W).astype(_F32)
    col = jnp.tile(jnp.arange(GRID_W), rows).astype(_F32)
    freqs = ROPE_THETA ** (-jnp.arange(ROPE_PAIRS, dtype=_F32) / ROPE_PAIRS)
    ang = jnp.stack([row[:, None] * freqs, col[:, None] * freqs], axis=1)
    cos, sin = jnp.cos(ang), jnp.sin(ang)
    cos_h = jnp.stack([cos, cos], axis=2).reshape(t, HEAD_DIM)
    sin_h = jnp.stack([-sin, sin], axis=2).reshape(t, HEAD_DIM)
    return cos_h, sin_h


def _rope_tables(base, gains, scale):
    cos_h, sin_h = base
    g = gains.astype(_F32)
    g_sw = g.reshape(-1, 2, 2, ROPE_PAIRS)[:, :, ::-1, :].reshape(-1, HEAD_DIM)
    cos_t = cos_h[None] * (g * scale)[:, None, :]
    sin_t = sin_h[None] * (g_sw * scale)[:, None, :]
    reps = (1, 1, LANES // HEAD_DIM)
    return jnp.tile(cos_t, reps), jnp.tile(sin_t, reps)


def _pool_inv_counts(t):
    pos = jnp.arange(t)[:, None]
    w = jnp.repeat(jnp.array(POOL_WINDOWS), POOL_GROUP)[None, :]
    cnt = jnp.minimum(pos + (w - w // 2), t) - jnp.maximum(pos - w // 2, 0)
    return 1.0 / cnt.astype(_F32)


def _trunk(x, mem, prep):
    pool_inv = _pool_inv_counts(x.shape[1])
    for layer in range(DEPTH):
        u, gate, qt, qxt, k, vt = _proj(layer, x, prep["norm_pre"], prep["w_in"], prep["seg"],
                                        *prep["q_tables"], *prep["k_tables"])
        x = _mix(layer, qt, qxt, k, vt, mem, prep["mem_norm"], prep["w_mem_kv"], gate, u, pool_inv, x,
                 prep["w_pool"], prep["pool_scale"], prep["w_out"], prep["norm_post"])
    return x


def kernel(x_prompt, x_sample, mem_prompt, mem_sample, norm_pre, norm_post, w_in, pool_w, pool_scale,
           q_norm, k_norm, mem_norm, w_mem_kv, w_out):
    n_win = len(POOL_WINDOWS)
    eye = jnp.eye(n_win, dtype=pool_w.dtype)
    base = _rope_base(max(x_prompt.shape[1], x_sample.shape[1]))
    prep = {
        "norm_pre": norm_pre[:, None, :], "norm_post": norm_post[:, None, :],
        "mem_norm": mem_norm[:, None, :], "pool_scale": pool_scale[:, None, :],
        "w_in": w_in.astype(_BF16), "w_out": w_out.astype(_BF16), "w_mem_kv": w_mem_kv.astype(_BF16),
        "w_pool": jnp.einsum("gh,lgcd->lgchd", eye, pool_w).reshape(DEPTH, POOL_W, POOL_W).astype(_BF16),
        "seg": jnp.kron(jnp.eye(MXU_WIDTH // HEAD_DIM, dtype=_F32),
                        jnp.ones((HEAD_DIM, HEAD_DIM), _F32)).astype(_BF16),
        "q_tables": _rope_tables(base, q_norm, Q_SCALE),
        "k_tables": _rope_tables(base, k_norm, 1.0),
    }
    return (_trunk(x_prompt, mem_prompt, prep), _trunk(x_sample, mem_sample, prep))
```

```python
import math

import jax
import jax.numpy as jnp
from jax import lax
from jax.experimental import pallas as pl
from jax.experimental.pallas import tpu as pltpu

D_MODEL = 1024
DEPTH = 2
GRID_W = 64
N_MEM = 256
HEAD_DIM = 64
N_HEADS = 8
N_KV_HEADS = 2
ATTN_W = N_HEADS * HEAD_DIM
KV_W = N_KV_HEADS * HEAD_DIM
POOL_W = 256
POOL_WINDOWS = (2, 4, 8, 16)
POOL_GROUP = POOL_W // len(POOL_WINDOWS)
N_XHEADS = 4
XATTN_W = N_XHEADS * HEAD_DIM
MIX_W = POOL_W + ATTN_W + XATTN_W
IN_W = 2 * POOL_W + 2 * ATTN_W + 2 * KV_W + 2 * XATTN_W
ROPE_THETA = 10000.0
ROPE_PAIRS = HEAD_DIM // 4
EPS = 1e-6
Q_SCALE = math.log2(math.e) * HEAD_DIM ** -0.5

LANES = 128
MXU_WIDTH = 256
BF16_SUBLANES = 16
POOL_HALO = 8
PROJ_ROWS = 1024
PROJ_SUB = 512
KEY_CHUNK = 256
Q_ROWS_CANDIDATES = (512, 256)
MIX_TEMP_BYTES = 12 * 1024 * 1024
Q_SUB = 256
PAIRS_PER_STREAM = 2
ATTN_LOOKAHEAD = 2
VMEM_LIMIT_BYTES = 56 * 1024 * 1024

VT_ROWS = HEAD_DIM + BF16_SUBLANES

_OFF_U = 0
_OFF_GP = POOL_W
_OFF_Q = 2 * POOL_W
_OFF_K = _OFF_Q + ATTN_W
_OFF_V = _OFF_K + KV_W
_OFF_GA = _OFF_V + KV_W
_OFF_QX = _OFF_GA + ATTN_W
_OFF_GX = _OFF_QX + XATTN_W

_BF16 = jnp.bfloat16
_F32 = jnp.float32


def _dot(a, b):
    return jnp.dot(a, b, preferred_element_type=_F32)


def _silu(g):
    half = 0.5 * g
    return half + half * jnp.tanh(half)


def _is_even_head_lane(shape):
    return (lax.broadcasted_iota(jnp.int32, shape, len(shape) - 1) % LANES) < HEAD_DIM


def _ones_row_tile(cols):
    row = lax.broadcasted_iota(jnp.int32, (BF16_SUBLANES, cols), 0)
    return jnp.where(row == 0, 1.0, 0.0).astype(_BF16)


def _memory_kv(mem_ref, g_ref, w_ref, k_ref, vt_ref):
    m = mem_ref[0]
    ms = jnp.mean(m * m, axis=-1, keepdims=True)
    mh = (m * lax.rsqrt(ms + EPS) * g_ref[...]).astype(_BF16)
    kv = _dot(mh, w_ref[...])
    k = kv[:, :XATTN_W]
    v_t = kv[:, XATTN_W:].T
    left = _is_even_head_lane((N_MEM, LANES))
    for pair in range(N_XHEADS // 2):
        kp = k[:, pair * LANES:(pair + 1) * LANES]
        k_ref[2 * pair] = jnp.where(left, kp, 0.0).astype(_BF16)
        k_ref[2 * pair + 1] = jnp.where(left, 0.0, kp).astype(_BF16)
    for head in range(N_XHEADS):
        vt_ref[head, 0:HEAD_DIM] = v_t[head * HEAD_DIM:(head + 1) * HEAD_DIM].astype(_BF16)
        vt_ref[head, HEAD_DIM:VT_ROWS] = _ones_row_tile(N_MEM)


def _swap16(x):
    n = x.shape[-1]
    lane = lax.broadcasted_iota(jnp.int32, x.shape, 1)
    first = (lane & ROPE_PAIRS) == 0
    return jnp.where(first, pltpu.roll(x, n - ROPE_PAIRS, 1), pltpu.roll(x, ROPE_PAIRS, 1))


def _norm_rope(z, seg, cos_tab, sin_tab):
    reps = z.shape[1] // LANES
    cos_tab = jnp.concatenate([cos_tab] * reps, axis=1)
    sin_tab = jnp.concatenate([sin_tab] * reps, axis=1)
    ssq = _dot((z * z).astype(_BF16), seg)
    r = lax.rsqrt(ssq * (1.0 / HEAD_DIM) + EPS)
    return r * (z * cos_tab + _swap16(z) * sin_tab)


def _proj_kernel(x_ref, g_ref, w_ref, seg_ref, cq_ref, sq_ref, ck_ref, sk_ref,
                 u_ref, gate_ref, qt_ref, qxt_ref, k_ref, vt_ref):
    seg = seg_ref[...]
    seg_w = seg.shape[0]
    for r0 in range(0, x_ref.shape[1], PROJ_SUB):
        rs = slice(r0, r0 + PROJ_SUB)
        x = x_ref[0, rs, :]
        ms = jnp.mean(x * x, axis=-1, keepdims=True)
        h = (x * lax.rsqrt(ms + EPS) * g_ref[...]).astype(_BF16)

        z_qkv = _dot(h, w_ref[:, _OFF_Q:_OFF_GA])
        z_pool = _dot(h, w_ref[:, _OFF_U:_OFF_Q])

        cq, sq = cq_ref[rs, :], sq_ref[rs, :]
        for lo in range(0, ATTN_W, seg_w):
            q = _norm_rope(z_qkv[:, lo:lo + seg_w], seg, cq, sq)
            for pair in range(lo // LANES, (lo + seg_w) // LANES):
                qt_ref[0, pair, :, rs] = q[:, pair * LANES - lo:(pair + 1) * LANES - lo].T.astype(_BF16)

        k = _norm_rope(z_qkv[:, ATTN_W:ATTN_W + KV_W], seg[:KV_W, :KV_W],
                       ck_ref[rs, :], sk_ref[rs, :])
        k_sw = pltpu.roll(k, HEAD_DIM, 1)
        left = _is_even_head_lane(k.shape)
        k_ref[0, 0, rs, :] = jnp.where(left, k, 0.0).astype(_BF16)
        k_ref[0, 1, rs, :] = jnp.where(left, 0.0, k_sw).astype(_BF16)
        k_ref[0, 2, rs, :] = jnp.where(left, k_sw, 0.0).astype(_BF16)
        k_ref[0, 3, rs, :] = jnp.where(left, 0.0, k).astype(_BF16)

        v_t = z_qkv[:, ATTN_W + KV_W:].T
        for head in range(N_KV_HEADS):
            vt_ref[0, head, 0:HEAD_DIM, rs] = v_t[head * HEAD_DIM:(head + 1) * HEAD_DIM].astype(_BF16)
            vt_ref[0, head, HEAD_DIM:VT_ROWS, rs] = _ones_row_tile(PROJ_SUB)

        u_ref[0, rs, :] = z_pool[:, 0:POOL_W]
        gate_ref[0, rs, 0:POOL_W] = _silu(z_pool[:, POOL_W:]).astype(_BF16)

        z_rest = _dot(h, w_ref[:, _OFF_GA:])
        gate_ref[0, rs, POOL_W:POOL_W + ATTN_W] = _silu(z_rest[:, 0:ATTN_W]).astype(_BF16)
        gate_ref[0, rs, POOL_W + ATTN_W:MIX_W] = _silu(z_rest[:, ATTN_W + XATTN_W:]).astype(_BF16)
        for pair in range(N_XHEADS // 2):
            lo = ATTN_W + pair * LANES
            qxt_ref[0, pair, :, rs] = (z_rest[:, lo:lo + LANES] * Q_SCALE).T.astype(_BF16)


def _proj(layer, x, g, w, seg, cq, sq, ck, sk):
    b, t, _ = x.shape
    rows = PROJ_ROWS
    assert t % rows == 0 and rows % PROJ_SUB == 0 and t % GRID_W == 0, (t, rows)
    n = t // rows
    tab = pl.BlockSpec((None, rows, LANES), lambda i, j: (layer, j, 0))
    return pl.pallas_call(
        _proj_kernel,
        grid=(b, n),
        in_specs=[
            pl.BlockSpec((1, rows, D_MODEL), lambda i, j: (i, j, 0)),
            pl.BlockSpec((None, 1, D_MODEL), lambda i, j: (layer, 0, 0)),
            pl.BlockSpec((None, D_MODEL, IN_W), lambda i, j: (layer, 0, 0)),
            pl.BlockSpec((MXU_WIDTH, MXU_WIDTH), lambda i, j: (0, 0)),
            tab, tab, tab, tab,
        ],
        out_specs=[
            pl.BlockSpec((1, rows, POOL_W), lambda i, j: (i, j, 0)),
            pl.BlockSpec((1, rows, MIX_W), lambda i, j: (i, j, 0)),
            pl.BlockSpec((1, N_HEADS // 2, LANES, rows), lambda i, j: (i, 0, 0, j)),
            pl.BlockSpec((1, N_XHEADS // 2, LANES, rows), lambda i, j: (i, 0, 0, j)),
            pl.BlockSpec((1, 2 * N_KV_HEADS, rows, KV_W), lambda i, j: (i, 0, j, 0)),
            pl.BlockSpec((1, N_KV_HEADS, VT_ROWS, rows), lambda i, j: (i, 0, 0, j)),
        ],
        out_shape=[
            jax.ShapeDtypeStruct((b, t, POOL_W), _F32),
            jax.ShapeDtypeStruct((b, t, MIX_W), _BF16),
            jax.ShapeDtypeStruct((b, N_HEADS // 2, LANES, t), _BF16),
            jax.ShapeDtypeStruct((b, N_XHEADS // 2, LANES, t), _BF16),
            jax.ShapeDtypeStruct((b, 2 * N_KV_HEADS, t, KV_W), _BF16),
            jax.ShapeDtypeStruct((b, N_KV_HEADS, VT_ROWS, t), _BF16),
        ],
        compiler_params=pltpu.CompilerParams(
            dimension_semantics=("arbitrary", "arbitrary"), vmem_limit_bytes=VMEM_LIMIT_BYTES),
        name="proj",
    )(x, g, w, seg, cq, sq, ck, sk)


def _attend(streams, lookahead):
    n_chunks = [s[1].shape[1] // s[5] for s in streams]
    units = [(c, i, parity) for c in range(max(n_chunks)) for i in range(len(streams))
             if c < n_chunks[i] for parity in range(2)]
    units = [u for u in units if n_chunks[u[1]] > 1] + [u for u in units if n_chunks[u[1]] == 1]
    order = []
    for n, unit in enumerate(units):
        order.append(("scores", unit))
        if n >= lookahead:
            order.append(("finish", units[n - lookahead]))
    order += [("finish", unit) for unit in units[max(len(units) - lookahead, 0):]]
    state = [[None, None] for _ in streams]
    scores = {}

    def issue_scores(c, i, parity):
        q_t, k_ref, k_slot = streams[i][:3]
        chunk = streams[i][5]
        return _dot(k_ref[k_slot + parity, c * chunk:(c + 1) * chunk, :], q_t)

    def finish(c, i, parity, s):
        _, _, _, vt_ref, v_slots, chunk, sink = streams[i]
        mn = jnp.max(s, axis=0, keepdims=True)
        if state[i][parity] is not None:
            m, acc = state[i][parity]
            mn = jnp.maximum(m, mn)
        p = jnp.exp2((s - mn).astype(_BF16))
        pv = _dot(vt_ref[v_slots[parity], :, c * chunk:(c + 1) * chunk], p)
        if state[i][parity] is not None:
            pv = acc * jnp.exp2(m - mn) + pv
        state[i][parity] = (mn, pv)
        if c == n_chunks[i] - 1 and parity == 1:
            out_t = [acc[0:HEAD_DIM] * (1.0 / acc[HEAD_DIM:HEAD_DIM + 1]) for _, acc in state[i]]
            sink(jnp.concatenate(out_t, axis=0).T)

    for what, unit in order:
        if what == "scores":
            scores[unit] = issue_scores(*unit)
        else:
            finish(*unit, scores.pop(unit))


def _pool_delta(u_ref, up_ref, un_ref, inv_ref):
    tq = u_ref.shape[1]
    n_ext = tq + 2 * POOL_HALO
    is_first = pl.program_id(1) == 0
    is_last = pl.program_id(1) == pl.num_programs(1) - 1
    left = _is_even_head_lane((tq, LANES))
    deltas = []
    for half in range(POOL_W // LANES):
        w_a, w_b = POOL_WINDOWS[2 * half], POOL_WINDOWS[2 * half + 1]
        cols = slice(half * LANES, (half + 1) * LANES)
        u = u_ref[0, :, cols]
        ext = jnp.concatenate([jnp.where(is_first, 0.0, up_ref[0, :, cols]), u,
                               jnp.where(is_last, 0.0, un_ref[0, :, cols])], axis=0)
        sums = {}
        run = ext
        for w in POOL_WINDOWS[:2 * half + 2]:
            run = run + pltpu.roll(run, n_ext - w // 2, 0)
            if w in (w_a, w_b):
                aligned = run if w // 2 == POOL_HALO else pltpu.roll(run, w // 2, 0)
                first = 0 if w // 2 == POOL_HALO else POOL_HALO
                sums[w] = aligned[first:first + tq]
        mean = jnp.where(left, sums[w_a], sums[w_b]) * inv_ref[:, cols]
        deltas.append(mean - u)
    return jnp.concatenate(deltas, axis=1)


def _mix_kernel(qt_ref, qxt_ref, k_ref, vt_ref, mem_ref, gmem_ref, wmem_ref, gate_ref, u_ref, up_ref,
                un_ref, inv_ref, x_ref, wpool_ref, pscale_ref, wout_ref, gpost_ref, y_ref,
                mix_ref, km_ref, vmt_ref):
    @pl.when(pl.program_id(1) == 0)
    def _():
        _memory_kv(mem_ref, gmem_ref, wmem_ref, km_ref, vmt_ref)

    d = _pool_delta(u_ref, up_ref, un_ref, inv_ref).astype(_BF16)
    pool = _dot(d, wpool_ref[...]) * pscale_ref[...]
    mix_ref[:, 0:POOL_W] = (pool * gate_ref[0, :, 0:POOL_W].astype(_F32)).astype(_BF16)

    tq = qt_ref.shape[3]
    pairs_per_group = N_HEADS // N_KV_HEADS // 2

    def gated_sink(r0, rows, first_cols):
        def sink(out):
            for n, c0 in enumerate(first_cols):
                gate = gate_ref[0, r0:r0 + rows, c0:c0 + LANES].astype(_F32)
                mix_ref[r0:r0 + rows, c0:c0 + LANES] = (out[n * rows:(n + 1) * rows] * gate).astype(_BF16)
        return sink

    streams = []
    for pair in range(N_XHEADS // 2):
        streams.append((qxt_ref[0, pair], km_ref, 2 * pair, vmt_ref, (2 * pair, 2 * pair + 1), N_MEM,
                        gated_sink(0, tq, [POOL_W + ATTN_W + pair * LANES])))
    for r0 in range(0, tq, Q_SUB):
        cols = slice(r0, r0 + Q_SUB)
        for first in range(0, N_HEADS // 2, PAIRS_PER_STREAM):
            group = first // pairs_per_group
            pairs = list(range(first, first + PAIRS_PER_STREAM))
            q_t = jnp.concatenate([qt_ref[0, pair, :, cols] for pair in pairs], axis=1)
            streams.append((q_t, k_ref.at[0], 2 * group, vt_ref.at[0], (group, group), KEY_CHUNK,
                            gated_sink(r0, Q_SUB, [POOL_W + pair * LANES for pair in pairs])))

    _attend(streams, ATTN_LOOKAHEAD)

    y = _dot(mix_ref[...], wout_ref[...])
    ms = jnp.mean(y * y, axis=-1, keepdims=True)
    y_ref[0] = x_ref[0] + y * lax.rsqrt(ms + EPS) * gpost_ref[...]


def _query_rows(t):
    f32, bf16, buffers = 4, 2, 2
    resident = buffers * bf16 * t * (2 * N_KV_HEADS * KV_W + N_KV_HEADS * VT_ROWS)
    weights = buffers * (bf16 * (MIX_W * D_MODEL + D_MODEL * 2 * XATTN_W + POOL_W * POOL_W)
                         + f32 * N_MEM * D_MODEL)
    per_row = (buffers * (2 * f32 * D_MODEL + bf16 * (MIX_W + ATTN_W + XATTN_W) + 2 * f32 * POOL_W)
               + bf16 * MIX_W)
    for rows in Q_ROWS_CANDIDATES:
        if t % rows == 0 and resident + weights + rows * per_row + MIX_TEMP_BYTES <= VMEM_LIMIT_BYTES:
            return rows
    raise ValueError(f"no query block size fits VMEM for sequence length {t}")


def _mix(layer, qt, qxt, k, vt, mem, gmem, wmem, gate, u, pool_inv, x, wpool, pscale, wout, gpost):
    b, t, _ = x.shape
    tq = _query_rows(t)
    assert t % tq == 0 and tq % Q_SUB == 0 and t % KEY_CHUNK == 0, (t, tq)
    n = t // tq
    halo_per_q = tq // POOL_HALO
    n_halo = t // POOL_HALO
    return pl.pallas_call(
        _mix_kernel,
        grid=(b, n),
        in_specs=[
            pl.BlockSpec((1, N_HEADS // 2, LANES, tq), lambda i, j: (i, 0, 0, j)),
            pl.BlockSpec((1, N_XHEADS // 2, LANES, tq), lambda i, j: (i, 0, 0, j)),
            pl.BlockSpec((1, 2 * N_KV_HEADS, t, KV_W), lambda i, j: (i, 0, 0, 0)),
            pl.BlockSpec((1, N_KV_HEADS, VT_ROWS, t), lambda i, j: (i, 0, 0, 0)),
            pl.BlockSpec((1, N_MEM, D_MODEL), lambda i, j: (i, 0, 0)),
            pl.BlockSpec((None, 1, D_MODEL), lambda i, j: (layer, 0, 0)),
            pl.BlockSpec((None, D_MODEL, 2 * XATTN_W), lambda i, j: (layer, 0, 0)),
            pl.BlockSpec((1, tq, MIX_W), lambda i, j: (i, j, 0)),
            pl.BlockSpec((1, tq, POOL_W), lambda i, j: (i, j, 0)),
            pl.BlockSpec((1, POOL_HALO, POOL_W),
                         lambda i, j: (i, jnp.maximum(j * halo_per_q - 1, 0), 0)),
            pl.BlockSpec((1, POOL_HALO, POOL_W),
                         lambda i, j: (i, jnp.minimum((j + 1) * halo_per_q, n_halo - 1), 0)),
            pl.BlockSpec((tq, POOL_W), lambda i, j: (j, 0)),
            pl.BlockSpec((1, tq, D_MODEL), lambda i, j: (i, j, 0)),
            pl.BlockSpec((None, POOL_W, POOL_W), lambda i, j: (layer, 0, 0)),
            pl.BlockSpec((None, 1, POOL_W), lambda i, j: (layer, 0, 0)),
            pl.BlockSpec((None, MIX_W, D_MODEL), lambda i, j: (layer, 0, 0)),
            pl.BlockSpec((None, 1, D_MODEL), lambda i, j: (layer, 0, 0)),
        ],
        out_specs=pl.BlockSpec((1, tq, D_MODEL), lambda i, j: (i, j, 0)),
        out_shape=jax.ShapeDtypeStruct((b, t, D_MODEL), _F32),
        scratch_shapes=[
            pltpu.VMEM((tq, MIX_W), _BF16),
            pltpu.VMEM((N_XHEADS, N_MEM, LANES), _BF16),
            pltpu.VMEM((N_XHEADS, VT_ROWS, N_MEM), _BF16),
        ],
        compiler_params=pltpu.CompilerParams(
            dimension_semantics=("arbitrary", "arbitrary"), vmem_limit_bytes=VMEM_LIMIT_BYTES),
        name="mix",
    )(qt, qxt, k, vt, mem, gmem, wmem, gate, u, u, u, pool_inv, x, wpool, pscale, wout, gpost)


def _rope_base(t):
    rows = t // GRID_W
    row = jnp.repeat(jnp.arange(rows), GRID_W).astype(_F32)
    col = jnp.tile(jnp.arange(GRID_W), rows).astype(_F32)
    freqs = ROPE_THETA ** (-jnp.arange(ROPE_PAIRS, dtype=_F32) / ROPE_PAIRS)
    ang = jnp.stack([row[:, None] * freqs, col[:, None] * freqs], axis=1)
    cos, sin = jnp.cos(ang), jnp.sin(ang)
    cos_h = jnp.stack([cos, cos], axis=2).reshape(t, HEAD_DIM)
    sin_h = jnp.stack([-sin, sin], axis=2).reshape(t, HEAD_DIM)
    return cos_h, sin_h


def _rope_tables(base, gains, scale):
    cos_h, sin_h = base
    g = gains.astype(_F32)
    g_sw = g.reshape(-1, 2, 2, ROPE_PAIRS)[:, :, ::-1, :].reshape(-1, HEAD_DIM)
    cos_t = cos_h[None] * (g * scale)[:, None, :]
    sin_t = sin_h[None] * (g_sw * scale)[:, None, :]
    reps = (1, 1, LANES // HEAD_DIM)
    return jnp.tile(cos_t, reps), jnp.tile(sin_t, reps)


def _pool_inv_counts(t):
    pos = jnp.arange(t)[:, None]
    w = jnp.repeat(jnp.array(POOL_WINDOWS), POOL_GROUP)[None, :]
    cnt = jnp.minimum(pos + (w - w // 2), t) - jnp.maximum(pos - w // 2, 0)
    return 1.0 / cnt.astype(_F32)


def _trunk(x, mem, prep):
    pool_inv = _pool_inv_counts(x.shape[1])
    for layer in range(DEPTH):
        u, gate, qt, qxt, k, vt = _proj(layer, x, prep["norm_pre"], prep["w_in"], prep["seg"],
                                        *prep["q_tables"], *prep["k_tables"])
        x = _mix(layer, qt, qxt, k, vt, mem, prep["mem_norm"], prep["w_mem_kv"], gate, u, pool_inv, x,
                 prep["w_pool"], prep["pool_scale"], prep["w_out"], prep["norm_post"])
    return x


def kernel(x_prompt, x_sample, mem_prompt, mem_sample, norm_pre, norm_post, w_in, pool_w, pool_scale,
           q_norm, k_norm, mem_norm, w_mem_kv, w_out):
    n_win = len(POOL_WINDOWS)
    eye = jnp.eye(n_win, dtype=pool_w.dtype)
    base = _rope_base(max(x_prompt.shape[1], x_sample.shape[1]))
    prep = {
        "norm_pre": norm_pre[:, None, :], "norm_post": norm_post[:, None, :],
        "mem_norm": mem_norm[:, None, :], "pool_scale": pool_scale[:, None, :],
        "w_in": w_in.astype(_BF16), "w_out": w_out.astype(_BF16), "w_mem_kv": w_mem_kv.astype(_BF16),
        "w_pool": jnp.einsum("gh,lgcd->lgchd", eye, pool_w).reshape(DEPTH, POOL_W, POOL_W).astype(_BF16),
        "seg": jnp.kron(jnp.eye(MXU_WIDTH // HEAD_DIM, dtype=_F32),
                        jnp.ones((HEAD_DIM, HEAD_DIM), _F32)).astype(_BF16),
        "q_tables": _rope_tables(base, q_norm, Q_SCALE),
        "k_tables": _rope_tables(base, k_norm, 1.0),
    }
    return (_trunk(x_prompt, mem_prompt, prep), _trunk(x_sample, mem_sample, prep))
```

```python
import math

import jax
import jax.numpy as jnp
from jax import lax
from jax.experimental import pallas as pl
from jax.experimental.pallas import tpu as pltpu

D_MODEL = 1024
DEPTH = 2
GRID_W = 64
N_MEM = 256
HEAD_DIM = 64
N_HEADS = 8
N_KV_HEADS = 2
ATTN_W = N_HEADS * HEAD_DIM
KV_W = N_KV_HEADS * HEAD_DIM
POOL_W = 256
POOL_WINDOWS = (2, 4, 8, 16)
POOL_GROUP = POOL_W // len(POOL_WINDOWS)
N_XHEADS = 4
XATTN_W = N_XHEADS * HEAD_DIM
MIX_W = POOL_W + ATTN_W + XATTN_W
IN_W = 2 * POOL_W + 2 * ATTN_W + 2 * KV_W + 2 * XATTN_W
ROPE_THETA = 10000.0
ROPE_PAIRS = HEAD_DIM // 4
EPS = 1e-6
Q_SCALE = math.log2(math.e) * HEAD_DIM ** -0.5

LANES = 128
MXU_WIDTH = 256
BF16_SUBLANES = 16
POOL_HALO = 8
PROJ_ROWS = 1024
PROJ_SUB = 512
KEY_CHUNK = 256
Q_ROWS_CANDIDATES = (1024, 512)
X_SUB = 512
MIX_TEMP_BYTES = 12 * 1024 * 1024
Q_SUB = 256
PAIRS_PER_STREAM = 2
ATTN_LOOKAHEAD = 2
VMEM_LIMIT_BYTES = 56 * 1024 * 1024

VT_ROWS = HEAD_DIM + BF16_SUBLANES

_OFF_U = 0
_OFF_GP = POOL_W
_OFF_Q = 2 * POOL_W
_OFF_K = _OFF_Q + ATTN_W
_OFF_V = _OFF_K + KV_W
_OFF_GA = _OFF_V + KV_W
_OFF_QX = _OFF_GA + ATTN_W
_OFF_GX = _OFF_QX + XATTN_W

_BF16 = jnp.bfloat16
_F32 = jnp.float32


def _dot(a, b):
    return jnp.dot(a, b, preferred_element_type=_F32)


def _silu(g):
    half = 0.5 * g
    return half + half * jnp.tanh(half)


def _is_even_head_lane(shape):
    return (lax.broadcasted_iota(jnp.int32, shape, len(shape) - 1) % LANES) < HEAD_DIM


def _ones_row_tile(cols):
    row = lax.broadcasted_iota(jnp.int32, (BF16_SUBLANES, cols), 0)
    return jnp.where(row == 0, 1.0, 0.0).astype(_BF16)


def _memory_kv(mem_ref, g_ref, w_ref, k_ref, vt_ref):
    m = mem_ref[0]
    ms = jnp.mean(m * m, axis=-1, keepdims=True)
    mh = (m * lax.rsqrt(ms + EPS) * g_ref[...]).astype(_BF16)
    kv = _dot(mh, w_ref[...])
    k = kv[:, :XATTN_W]
    v_t = kv[:, XATTN_W:].T
    left = _is_even_head_lane((N_MEM, LANES))
    for pair in range(N_XHEADS // 2):
        kp = k[:, pair * LANES:(pair + 1) * LANES]
        k_ref[2 * pair] = jnp.where(left, kp, 0.0).astype(_BF16)
        k_ref[2 * pair + 1] = jnp.where(left, 0.0, kp).astype(_BF16)
    for head in range(N_XHEADS):
        vt_ref[head, 0:HEAD_DIM] = v_t[head * HEAD_DIM:(head + 1) * HEAD_DIM].astype(_BF16)
        vt_ref[head, HEAD_DIM:VT_ROWS] = _ones_row_tile(N_MEM)


def _swap16(x):
    n = x.shape[-1]
    lane = lax.broadcasted_iota(jnp.int32, x.shape, 1)
    first = (lane & ROPE_PAIRS) == 0
    return jnp.where(first, pltpu.roll(x, n - ROPE_PAIRS, 1), pltpu.roll(x, ROPE_PAIRS, 1))


def _norm_rope(z, seg, cos_tab, sin_tab):
    reps = z.shape[1] // LANES
    cos_tab = jnp.concatenate([cos_tab] * reps, axis=1)
    sin_tab = jnp.concatenate([sin_tab] * reps, axis=1)
    ssq = _dot((z * z).astype(_BF16), seg)
    r = lax.rsqrt(ssq * (1.0 / HEAD_DIM) + EPS)
    return r * (z * cos_tab + _swap16(z) * sin_tab)


def _proj_kernel(x_ref, g_ref, w_ref, seg_ref, cq_ref, sq_ref, ck_ref, sk_ref,
                 u_ref, gate_ref, qt_ref, qxt_ref, k_ref, vt_ref):
    seg = seg_ref[...]
    seg_w = seg.shape[0]
    for r0 in range(0, x_ref.shape[1], PROJ_SUB):
        rs = slice(r0, r0 + PROJ_SUB)
        x = x_ref[0, rs, :]
        ms = jnp.mean(x * x, axis=-1, keepdims=True)
        h = (x * lax.rsqrt(ms + EPS) * g_ref[...]).astype(_BF16)

        z_qkv = _dot(h, w_ref[:, _OFF_Q:_OFF_GA])
        z_pool = _dot(h, w_ref[:, _OFF_U:_OFF_Q])

        cq, sq = cq_ref[rs, :], sq_ref[rs, :]
        for lo in range(0, ATTN_W, seg_w):
            q = _norm_rope(z_qkv[:, lo:lo + seg_w], seg, cq, sq)
            for pair in range(lo // LANES, (lo + seg_w) // LANES):
                qt_ref[0, pair, :, rs] = q[:, pair * LANES - lo:(pair + 1) * LANES - lo].T.astype(_BF16)

        k = _norm_rope(z_qkv[:, ATTN_W:ATTN_W + KV_W], seg[:KV_W, :KV_W],
                       ck_ref[rs, :], sk_ref[rs, :])
        k_sw = pltpu.roll(k, HEAD_DIM, 1)
        left = _is_even_head_lane(k.shape)
        k_ref[0, 0, rs, :] = jnp.where(left, k, 0.0).astype(_BF16)
        k_ref[0, 1, rs, :] = jnp.where(left, 0.0, k_sw).astype(_BF16)
        k_ref[0, 2, rs, :] = jnp.where(left, k_sw, 0.0).astype(_BF16)
        k_ref[0, 3, rs, :] = jnp.where(left, 0.0, k).astype(_BF16)

        v_t = z_qkv[:, ATTN_W + KV_W:].T
        for head in range(N_KV_HEADS):
            vt_ref[0, head, 0:HEAD_DIM, rs] = v_t[head * HEAD_DIM:(head + 1) * HEAD_DIM].astype(_BF16)
            vt_ref[0, head, HEAD_DIM:VT_ROWS, rs] = _ones_row_tile(PROJ_SUB)

        u_ref[0, rs, :] = z_pool[:, 0:POOL_W]
        gate_ref[0, rs, 0:POOL_W] = _silu(z_pool[:, POOL_W:]).astype(_BF16)

        z_rest = _dot(h, w_ref[:, _OFF_GA:])
        gate_ref[0, rs, POOL_W:POOL_W + ATTN_W] = _silu(z_rest[:, 0:ATTN_W]).astype(_BF16)
        gate_ref[0, rs, POOL_W + ATTN_W:MIX_W] = _silu(z_rest[:, ATTN_W + XATTN_W:]).astype(_BF16)
        for pair in range(N_XHEADS // 2):
            lo = ATTN_W + pair * LANES
            qxt_ref[0, pair, :, rs] = (z_rest[:, lo:lo + LANES] * Q_SCALE).T.astype(_BF16)


def _proj(layer, x, g, w, seg, cq, sq, ck, sk):
    b, t, _ = x.shape
    rows = PROJ_ROWS
    assert t % rows == 0 and rows % PROJ_SUB == 0 and t % GRID_W == 0, (t, rows)
    n = t // rows
    tab = pl.BlockSpec((None, rows, LANES), lambda i, j: (layer, j, 0))
    return pl.pallas_call(
        _proj_kernel,
        grid=(b, n),
        in_specs=[
            pl.BlockSpec((1, rows, D_MODEL), lambda i, j: (i, j, 0)),
            pl.BlockSpec((None, 1, D_MODEL), lambda i, j: (layer, 0, 0)),
            pl.BlockSpec((None, D_MODEL, IN_W), lambda i, j: (layer, 0, 0)),
            pl.BlockSpec((MXU_WIDTH, MXU_WIDTH), lambda i, j: (0, 0)),
            tab, tab, tab, tab,
        ],
        out_specs=[
            pl.BlockSpec((1, rows, POOL_W), lambda i, j: (i, j, 0)),
            pl.BlockSpec((1, rows, MIX_W), lambda i, j: (i, j, 0)),
            pl.BlockSpec((1, N_HEADS // 2, LANES, rows), lambda i, j: (i, 0, 0, j)),
            pl.BlockSpec((1, N_XHEADS // 2, LANES, rows), lambda i, j: (i, 0, 0, j)),
            pl.BlockSpec((1, 2 * N_KV_HEADS, rows, KV_W), lambda i, j: (i, 0, j, 0)),
            pl.BlockSpec((1, N_KV_HEADS, VT_ROWS, rows), lambda i, j: (i, 0, 0, j)),
        ],
        out_shape=[
            jax.ShapeDtypeStruct((b, t, POOL_W), _F32),
            jax.ShapeDtypeStruct((b, t, MIX_W), _BF16),
            jax.ShapeDtypeStruct((b, N_HEADS // 2, LANES, t), _BF16),
            jax.ShapeDtypeStruct((b, N_XHEADS // 2, LANES, t), _BF16),
            jax.ShapeDtypeStruct((b, 2 * N_KV_HEADS, t, KV_W), _BF16),
            jax.ShapeDtypeStruct((b, N_KV_HEADS, VT_ROWS, t), _BF16),
        ],
        compiler_params=pltpu.CompilerParams(
            dimension_semantics=("arbitrary", "arbitrary"), vmem_limit_bytes=VMEM_LIMIT_BYTES),
        name="proj",
    )(x, g, w, seg, cq, sq, ck, sk)


def _attend(streams, lookahead):
    n_chunks = [s[1].shape[1] // s[5] for s in streams]
    units = [(c, i, parity) for c in range(max(n_chunks)) for i in range(len(streams))
             if c < n_chunks[i] for parity in range(2)]
    units = [u for u in units if n_chunks[u[1]] > 1] + [u for u in units if n_chunks[u[1]] == 1]
    order = []
    for n, unit in enumerate(units):
        order.append(("scores", unit))
        if n >= lookahead:
            order.append(("finish", units[n - lookahead]))
    order += [("finish", unit) for unit in units[max(len(units) - lookahead, 0):]]
    state = [[None, None] for _ in streams]
    scores = {}

    def issue_scores(c, i, parity):
        q_t, k_ref, k_slot = streams[i][:3]
        chunk = streams[i][5]
        return _dot(k_ref[k_slot + parity, c * chunk:(c + 1) * chunk, :], q_t)

    def finish(c, i, parity, s):
        _, _, _, vt_ref, v_slots, chunk, sink = streams[i]
        mn = jnp.max(s, axis=0, keepdims=True)
        if state[i][parity] is not None:
            m, acc = state[i][parity]
            mn = jnp.maximum(m, mn)
        p = jnp.exp2((s - mn).astype(_BF16))
        pv = _dot(vt_ref[v_slots[parity], :, c * chunk:(c + 1) * chunk], p)
        if state[i][parity] is not None:
            pv = acc * jnp.exp2(m - mn) + pv
        state[i][parity] = (mn, pv)
        if c == n_chunks[i] - 1 and parity == 1:
            out_t = [acc[0:HEAD_DIM] * (1.0 / acc[HEAD_DIM:HEAD_DIM + 1]) for _, acc in state[i]]
            sink(jnp.concatenate(out_t, axis=0).T)

    for what, unit in order:
        if what == "scores":
            scores[unit] = issue_scores(*unit)
        else:
            finish(*unit, scores.pop(unit))


def _pool_delta(u_ref, up_ref, un_ref, inv_ref):
    tq = u_ref.shape[1]
    n_ext = tq + 2 * POOL_HALO
    is_first = pl.program_id(1) == 0
    is_last = pl.program_id(1) == pl.num_programs(1) - 1
    left = _is_even_head_lane((tq, LANES))
    deltas = []
    for half in range(POOL_W // LANES):
        w_a, w_b = POOL_WINDOWS[2 * half], POOL_WINDOWS[2 * half + 1]
        cols = slice(half * LANES, (half + 1) * LANES)
        u = u_ref[0, :, cols]
        ext = jnp.concatenate([jnp.where(is_first, 0.0, up_ref[0, :, cols]), u,
                               jnp.where(is_last, 0.0, un_ref[0, :, cols])], axis=0)
        sums = {}
        run = ext
        for w in POOL_WINDOWS[:2 * half + 2]:
            run = run + pltpu.roll(run, n_ext - w // 2, 0)
            if w in (w_a, w_b):
                aligned = run if w // 2 == POOL_HALO else pltpu.roll(run, w // 2, 0)
                first = 0 if w // 2 == POOL_HALO else POOL_HALO
                sums[w] = aligned[first:first + tq]
        mean = jnp.where(left, sums[w_a], sums[w_b]) * inv_ref[:, cols]
        deltas.append(mean - u)
    return jnp.concatenate(deltas, axis=1)


def _mix_kernel(qt_ref, qxt_ref, k_ref, vt_ref, mem_ref, gmem_ref, wmem_ref, gate_ref, u_ref, up_ref,
                un_ref, inv_ref, x_ref, wpool_ref, pscale_ref, wout_ref, gpost_ref, y_ref,
                mix_ref, km_ref, vmt_ref):
    @pl.when(pl.program_id(1) == 0)
    def _():
        _memory_kv(mem_ref, gmem_ref, wmem_ref, km_ref, vmt_ref)

    d = _pool_delta(u_ref, up_ref, un_ref, inv_ref).astype(_BF16)
    pool = _dot(d, wpool_ref[...]) * pscale_ref[...]
    mix_ref[:, 0:POOL_W] = (pool * gate_ref[0, :, 0:POOL_W].astype(_F32)).astype(_BF16)

    tq = qt_ref.shape[3]
    pairs_per_group = N_HEADS // N_KV_HEADS // 2

    def gated_sink(r0, rows, first_cols):
        def sink(out):
            for n, c0 in enumerate(first_cols):
                gate = gate_ref[0, r0:r0 + rows, c0:c0 + LANES].astype(_F32)
                mix_ref[r0:r0 + rows, c0:c0 + LANES] = (out[n * rows:(n + 1) * rows] * gate).astype(_BF16)
        return sink

    streams = []
    for r0 in range(0, tq, X_SUB):
        for pair in range(N_XHEADS // 2):
            streams.append((qxt_ref[0, pair, :, r0:r0 + X_SUB], km_ref, 2 * pair,
                            vmt_ref, (2 * pair, 2 * pair + 1), N_MEM,
                            gated_sink(r0, X_SUB, [POOL_W + ATTN_W + pair * LANES])))
    for r0 in range(0, tq, Q_SUB):
        cols = slice(r0, r0 + Q_SUB)
        for first in range(0, N_HEADS // 2, PAIRS_PER_STREAM):
            group = first // pairs_per_group
            pairs = list(range(first, first + PAIRS_PER_STREAM))
            q_t = jnp.concatenate([qt_ref[0, pair, :, cols] for pair in pairs], axis=1)
            streams.append((q_t, k_ref.at[0], 2 * group, vt_ref.at[0], (group, group), KEY_CHUNK,
                            gated_sink(r0, Q_SUB, [POOL_W + pair * LANES for pair in pairs])))

    _attend(streams, ATTN_LOOKAHEAD)

    y = _dot(mix_ref[...], wout_ref[...])
    ms = jnp.mean(y * y, axis=-1, keepdims=True)
    y_ref[0] = x_ref[0] + y * lax.rsqrt(ms + EPS) * gpost_ref[...]


def _query_rows(t):
    f32, bf16, buffers = 4, 2, 2
    resident = buffers * bf16 * t * (2 * N_KV_HEADS * KV_W + N_KV_HEADS * VT_ROWS)
    weights = buffers * (bf16 * (MIX_W * D_MODEL + D_MODEL * 2 * XATTN_W + POOL_W * POOL_W)
                         + f32 * N_MEM * D_MODEL)
    per_row = (buffers * (2 * f32 * D_MODEL + bf16 * (MIX_W + ATTN_W + XATTN_W) + 2 * f32 * POOL_W)
               + bf16 * MIX_W)
    for rows in Q_ROWS_CANDIDATES:
        if t % rows == 0 and resident + weights + rows * per_row + MIX_TEMP_BYTES <= VMEM_LIMIT_BYTES:
            return rows
    raise ValueError(f"no query block size fits VMEM for sequence length {t}")


def _mix(layer, qt, qxt, k, vt, mem, gmem, wmem, gate, u, pool_inv, x, wpool, pscale, wout, gpost):
    b, t, _ = x.shape
    tq = _query_rows(t)
    assert t % tq == 0 and tq % Q_SUB == 0 and tq % X_SUB == 0 and t % KEY_CHUNK == 0, (t, tq)
    n = t // tq
    halo_per_q = tq // POOL_HALO
    n_halo = t // POOL_HALO
    return pl.pallas_call(
        _mix_kernel,
        grid=(b, n),
        in_specs=[
            pl.BlockSpec((1, N_HEADS // 2, LANES, tq), lambda i, j: (i, 0, 0, j)),
            pl.BlockSpec((1, N_XHEADS // 2, LANES, tq), lambda i, j: (i, 0, 0, j)),
            pl.BlockSpec((1, 2 * N_KV_HEADS, t, KV_W), lambda i, j: (i, 0, 0, 0)),
            pl.BlockSpec((1, N_KV_HEADS, VT_ROWS, t), lambda i, j: (i, 0, 0, 0)),
            pl.BlockSpec((1, N_MEM, D_MODEL), lambda i, j: (i, 0, 0)),
            pl.BlockSpec((None, 1, D_MODEL), lambda i, j: (layer, 0, 0)),
            pl.BlockSpec((None, D_MODEL, 2 * XATTN_W), lambda i, j: (layer, 0, 0)),
            pl.BlockSpec((1, tq, MIX_W), lambda i, j: (i, j, 0)),
            pl.BlockSpec((1, tq, POOL_W), lambda i, j: (i, j, 0)),
            pl.BlockSpec((1, POOL_HALO, POOL_W),
                         lambda i, j: (i, jnp.maximum(j * halo_per_q - 1, 0), 0)),
            pl.BlockSpec((1, POOL_HALO, POOL_W),
                         lambda i, j: (i, jnp.minimum((j + 1) * halo_per_q, n_halo - 1), 0)),
            pl.BlockSpec((tq, POOL_W), lambda i, j: (j, 0)),
            pl.BlockSpec((1, tq, D_MODEL), lambda i, j: (i, j, 0)),
            pl.BlockSpec((None, POOL_W, POOL_W), lambda i, j: (layer, 0, 0)),
            pl.BlockSpec((None, 1, POOL_W), lambda i, j: (layer, 0, 0)),
            pl.BlockSpec((None, MIX_W, D_MODEL), lambda i, j: (layer, 0, 0)),
            pl.BlockSpec((None, 1, D_MODEL), lambda i, j: (layer, 0, 0)),
        ],
        out_specs=pl.BlockSpec((1, tq, D_MODEL), lambda i, j: (i, j, 0)),
        out_shape=jax.ShapeDtypeStruct((b, t, D_MODEL), _F32),
        scratch_shapes=[
            pltpu.VMEM((tq, MIX_W), _BF16),
            pltpu.VMEM((N_XHEADS, N_MEM, LANES), _BF16),
            pltpu.VMEM((N_XHEADS, VT_ROWS, N_MEM), _BF16),
        ],
        compiler_params=pltpu.CompilerParams(
            dimension_semantics=("arbitrary", "arbitrary"), vmem_limit_bytes=VMEM_LIMIT_BYTES),
        name="mix",
    )(qt, qxt, k, vt, mem, gmem, wmem, gate, u, u, u, pool_inv, x, wpool, pscale, wout, gpost)


def _rope_base(t):
    rows = t // GRID_W
    row = jnp.repeat(jnp.arange(rows), GRID_W).astype(_F32)
    col = jnp.tile(jnp.arange(GRID_W), rows).astype(_F32)
    freqs = ROPE_THETA ** (-jnp.arange(ROPE_PAIRS, dtype=_F32) / ROPE_PAIRS)
    ang = jnp.stack([row[:, None] * freqs, col[:, None] * freqs], axis=1)
    cos, sin = jnp.cos(ang), jnp.sin(ang)
    cos_h = jnp.stack([cos, cos], axis=2).reshape(t, HEAD_DIM)
    sin_h = jnp.stack([-sin, sin], axis=2).reshape(t, HEAD_DIM)
    return cos_h, sin_h


def _rope_tables(base, gains, scale):
    cos_h, sin_h = base
    g = gains.astype(_F32)
    g_sw = g.reshape(-1, 2, 2, ROPE_PAIRS)[:, :, ::-1, :].reshape(-1, HEAD_DIM)
    cos_t = cos_h[None] * (g * scale)[:, None, :]
    sin_t = sin_h[None] * (g_sw * scale)[:, None, :]
    reps = (1, 1, LANES // HEAD_DIM)
    return jnp.tile(cos_t, reps), jnp.tile(sin_t, reps)


def _pool_inv_counts(t):
    pos = jnp.arange(t)[:, None]
    w = jnp.repeat(jnp.array(POOL_WINDOWS), POOL_GROUP)[None, :]
    cnt = jnp.minimum(pos + (w - w // 2), t) - jnp.maximum(pos - w // 2, 0)
    return 1.0 / cnt.astype(_F32)


def _trunk(x, mem, prep):
    pool_inv = _pool_inv_counts(x.shape[1])
    for layer in range(DEPTH):
        u, gate, qt, qxt, k, vt = _proj(layer, x, prep["norm_pre"], prep["w_in"], prep["seg"],
                                        *prep["q_tables"], *prep["k_tables"])
        x = _mix(layer, qt, qxt, k, vt, mem, prep["mem_norm"], prep["w_mem_kv"], gate, u, pool_inv, x,
                 prep["w_pool"], prep["pool_scale"], prep["w_out"], prep["norm_post"])
    return x


def kernel(x_prompt, x_sample, mem_prompt, mem_sample, norm_pre, norm_post, w_in, pool_w, pool_scale,
           q_norm, k_norm, mem_norm, w_mem_kv, w_out):
    n_win = len(POOL_WINDOWS)
    eye = jnp.eye(n_win, dtype=pool_w.dtype)
    base = _rope_base(max(x_prompt.shape[1], x_sample.shape[1]))
    prep = {
        "norm_pre": norm_pre[:, None, :], "norm_post": norm_post[:, None, :],
        "mem_norm": mem_norm[:, None, :], "pool_scale": pool_scale[:, None, :],
        "w_in": w_in.astype(_BF16), "w_out": w_out.astype(_BF16), "w_mem_kv": w_mem_kv.astype(_BF16),
        "w_pool": jnp.einsum("gh,lgcd->lgchd", eye, pool_w).reshape(DEPTH, POOL_W, POOL_W).astype(_BF16),
        "seg": jnp.kron(jnp.eye(MXU_WIDTH // HEAD_DIM, dtype=_F32),
                        jnp.ones((HEAD_DIM, HEAD_DIM), _F32)).astype(_BF16),
        "q_tables": _rope_tables(base, q_norm, Q_SCALE),
        "k_tables": _rope_tables(base, k_norm, 1.0),
    }
    return (_trunk(x_prompt, mem_prompt, prep), _trunk(x_sample, mem_sample, prep))
```

```python
import functools
import math

import jax
import jax.numpy as jnp
from jax import lax
from jax.experimental import pallas as pl
from jax.experimental.pallas import tpu as pltpu

D_MODEL = 1024
DEPTH = 2
GRID_W = 64
N_MEM = 256
HEAD_DIM = 64
N_HEADS = 8
N_KV_HEADS = 2
ATTN_W = N_HEADS * HEAD_DIM
KV_W = N_KV_HEADS * HEAD_DIM
POOL_W = 256
POOL_WINDOWS = (2, 4, 8, 16)
POOL_GROUP = POOL_W // len(POOL_WINDOWS)
N_XHEADS = 4
XATTN_W = N_XHEADS * HEAD_DIM
MIX_W = POOL_W + ATTN_W + XATTN_W
IN_W = 2 * POOL_W + 2 * ATTN_W + 2 * KV_W + 2 * XATTN_W
ROPE_THETA = 10000.0
ROPE_PAIRS = HEAD_DIM // 4
EPS = 1e-6
Q_SCALE = math.log2(math.e) * HEAD_DIM ** -0.5
BOUNDED_SCORE_LIMIT = 40.0
BF16_NORM_MARGIN = 1.02

LANES = 128
MXU_WIDTH = 256
BF16_SUBLANES = 16
POOL_HALO = 8
PROJ_ROWS = 1024
PROJ_SUB = 512
KEY_CHUNK = 256
Q_ROWS_CANDIDATES = (1024, 512)
X_SUB = 512
MIX_TEMP_BYTES = 12 * 1024 * 1024
Q_SUB = 256
PAIRS_PER_STREAM = 2
ATTN_LOOKAHEAD = 2
VMEM_LIMIT_BYTES = 56 * 1024 * 1024

VT_ROWS = HEAD_DIM + BF16_SUBLANES

_OFF_U = 0
_OFF_GP = POOL_W
_OFF_Q = 2 * POOL_W
_OFF_K = _OFF_Q + ATTN_W
_OFF_V = _OFF_K + KV_W
_OFF_GA = _OFF_V + KV_W
_OFF_QX = _OFF_GA + ATTN_W
_OFF_GX = _OFF_QX + XATTN_W

_BF16 = jnp.bfloat16
_F32 = jnp.float32


def _dot(a, b):
    return jnp.dot(a, b, preferred_element_type=_F32)


def _silu(g):
    half = 0.5 * g
    return half + half * jnp.tanh(half)


def _is_even_head_lane(shape):
    return (lax.broadcasted_iota(jnp.int32, shape, len(shape) - 1) % LANES) < HEAD_DIM


def _ones_row_tile(cols):
    row = lax.broadcasted_iota(jnp.int32, (BF16_SUBLANES, cols), 0)
    return jnp.where(row == 0, 1.0, 0.0).astype(_BF16)


def _memory_kv(mem_ref, g_ref, w_ref, k_ref, vt_ref):
    m = mem_ref[0]
    ms = jnp.mean(m * m, axis=-1, keepdims=True)
    mh = (m * lax.rsqrt(ms + EPS) * g_ref[...]).astype(_BF16)
    kv = _dot(mh, w_ref[...])
    k = kv[:, :XATTN_W]
    v_t = kv[:, XATTN_W:].T
    left = _is_even_head_lane((N_MEM, LANES))
    for pair in range(N_XHEADS // 2):
        kp = k[:, pair * LANES:(pair + 1) * LANES]
        k_ref[2 * pair] = jnp.where(left, kp, 0.0).astype(_BF16)
        k_ref[2 * pair + 1] = jnp.where(left, 0.0, kp).astype(_BF16)
    for head in range(N_XHEADS):
        vt_ref[head, 0:HEAD_DIM] = v_t[head * HEAD_DIM:(head + 1) * HEAD_DIM].astype(_BF16)
        vt_ref[head, HEAD_DIM:VT_ROWS] = _ones_row_tile(N_MEM)


def _swap16(x):
    n = x.shape[-1]
    lane = lax.broadcasted_iota(jnp.int32, x.shape, 1)
    first = (lane & ROPE_PAIRS) == 0
    return jnp.where(first, pltpu.roll(x, n - ROPE_PAIRS, 1), pltpu.roll(x, ROPE_PAIRS, 1))


def _norm_rope(z, seg, cos_tab, sin_tab):
    reps = z.shape[1] // LANES
    cos_tab = jnp.concatenate([cos_tab] * reps, axis=1)
    sin_tab = jnp.concatenate([sin_tab] * reps, axis=1)
    ssq = _dot((z * z).astype(_BF16), seg)
    r = lax.rsqrt(ssq * (1.0 / HEAD_DIM) + EPS)
    return r * (z * cos_tab + _swap16(z) * sin_tab)


def _proj_kernel(x_ref, g_ref, w_ref, seg_ref, cq_ref, sq_ref, ck_ref, sk_ref,
                 u_ref, gate_ref, qt_ref, qxt_ref, k_ref, vt_ref):
    seg = seg_ref[...]
    seg_w = seg.shape[0]
    for r0 in range(0, x_ref.shape[1], PROJ_SUB):
        rs = slice(r0, r0 + PROJ_SUB)
        x = x_ref[0, rs, :]
        ms = jnp.mean(x * x, axis=-1, keepdims=True)
        h = (x * lax.rsqrt(ms + EPS) * g_ref[...]).astype(_BF16)

        z_qkv = _dot(h, w_ref[:, _OFF_Q:_OFF_GA])
        z_pool = _dot(h, w_ref[:, _OFF_U:_OFF_Q])

        cq, sq = cq_ref[rs, :], sq_ref[rs, :]
        for lo in range(0, ATTN_W, seg_w):
            q = _norm_rope(z_qkv[:, lo:lo + seg_w], seg, cq, sq)
            for pair in range(lo // LANES, (lo + seg_w) // LANES):
                qt_ref[0, pair, :, rs] = q[:, pair * LANES - lo:(pair + 1) * LANES - lo].T.astype(_BF16)

        k = _norm_rope(z_qkv[:, ATTN_W:ATTN_W + KV_W], seg[:KV_W, :KV_W],
                       ck_ref[rs, :], sk_ref[rs, :])
        k_sw = pltpu.roll(k, HEAD_DIM, 1)
        left = _is_even_head_lane(k.shape)
        k_ref[0, 0, rs, :] = jnp.where(left, k, 0.0).astype(_BF16)
        k_ref[0, 1, rs, :] = jnp.where(left, 0.0, k_sw).astype(_BF16)
        k_ref[0, 2, rs, :] = jnp.where(left, k_sw, 0.0).astype(_BF16)
        k_ref[0, 3, rs, :] = jnp.where(left, 0.0, k).astype(_BF16)

        v_t = z_qkv[:, ATTN_W + KV_W:].T
        for head in range(N_KV_HEADS):
            vt_ref[0, head, 0:HEAD_DIM, rs] = v_t[head * HEAD_DIM:(head + 1) * HEAD_DIM].astype(_BF16)
            vt_ref[0, head, HEAD_DIM:VT_ROWS, rs] = _ones_row_tile(PROJ_SUB)

        u_ref[0, rs, :] = z_pool[:, 0:POOL_W]
        gate_ref[0, rs, 0:POOL_W] = _silu(z_pool[:, POOL_W:]).astype(_BF16)

        z_rest = _dot(h, w_ref[:, _OFF_GA:])
        gate_ref[0, rs, POOL_W:POOL_W + ATTN_W] = _silu(z_rest[:, 0:ATTN_W]).astype(_BF16)
        gate_ref[0, rs, POOL_W + ATTN_W:MIX_W] = _silu(z_rest[:, ATTN_W + XATTN_W:]).astype(_BF16)
        for pair in range(N_XHEADS // 2):
            lo = ATTN_W + pair * LANES
            qxt_ref[0, pair, :, rs] = (z_rest[:, lo:lo + LANES] * Q_SCALE).T.astype(_BF16)


def _proj(layer, x, g, w, seg, cq, sq, ck, sk):
    b, t, _ = x.shape
    rows = PROJ_ROWS
    assert t % rows == 0 and rows % PROJ_SUB == 0 and t % GRID_W == 0, (t, rows)
    n = t // rows
    tab = pl.BlockSpec((None, rows, LANES), lambda i, j: (layer, j, 0))
    return pl.pallas_call(
        _proj_kernel,
        grid=(b, n),
        in_specs=[
            pl.BlockSpec((1, rows, D_MODEL), lambda i, j: (i, j, 0)),
            pl.BlockSpec((None, 1, D_MODEL), lambda i, j: (layer, 0, 0)),
            pl.BlockSpec((None, D_MODEL, IN_W), lambda i, j: (layer, 0, 0)),
            pl.BlockSpec((MXU_WIDTH, MXU_WIDTH), lambda i, j: (0, 0)),
            tab, tab, tab, tab,
        ],
        out_specs=[
            pl.BlockSpec((1, rows, POOL_W), lambda i, j: (i, j, 0)),
            pl.BlockSpec((1, rows, MIX_W), lambda i, j: (i, j, 0)),
            pl.BlockSpec((1, N_HEADS // 2, LANES, rows), lambda i, j: (i, 0, 0, j)),
            pl.BlockSpec((1, N_XHEADS // 2, LANES, rows), lambda i, j: (i, 0, 0, j)),
            pl.BlockSpec((1, 2 * N_KV_HEADS, rows, KV_W), lambda i, j: (i, 0, j, 0)),
            pl.BlockSpec((1, N_KV_HEADS, VT_ROWS, rows), lambda i, j: (i, 0, 0, j)),
        ],
        out_shape=[
            jax.ShapeDtypeStruct((b, t, POOL_W), _F32),
            jax.ShapeDtypeStruct((b, t, MIX_W), _BF16),
            jax.ShapeDtypeStruct((b, N_HEADS // 2, LANES, t), _BF16),
            jax.ShapeDtypeStruct((b, N_XHEADS // 2, LANES, t), _BF16),
            jax.ShapeDtypeStruct((b, 2 * N_KV_HEADS, t, KV_W), _BF16),
            jax.ShapeDtypeStruct((b, N_KV_HEADS, VT_ROWS, t), _BF16),
        ],
        compiler_params=pltpu.CompilerParams(
            dimension_semantics=("arbitrary", "arbitrary"), vmem_limit_bytes=VMEM_LIMIT_BYTES),
        name="proj",
    )(x, g, w, seg, cq, sq, ck, sk)


def _attend(streams, lookahead):
    n_chunks = [s[1].shape[1] // s[5] for s in streams]
    units = [(c, i, parity) for c in range(max(n_chunks)) for i in range(len(streams))
             if c < n_chunks[i] for parity in range(2)]
    units = [u for u in units if n_chunks[u[1]] > 1] + [u for u in units if n_chunks[u[1]] == 1]
    order = []
    for n, unit in enumerate(units):
        order.append(("scores", unit))
        if n >= lookahead:
            order.append(("finish", units[n - lookahead]))
    order += [("finish", unit) for unit in units[max(len(units) - lookahead, 0):]]
    state = [[None, None] for _ in streams]
    scores = {}

    def issue_scores(c, i, parity):
        q_t, k_ref, k_slot = streams[i][:3]
        chunk = streams[i][5]
        return _dot(k_ref[k_slot + parity, c * chunk:(c + 1) * chunk, :], q_t)

    def finish(c, i, parity, s):
        _, _, _, vt_ref, v_slots, chunk, sink, bounded = streams[i]
        values = vt_ref[v_slots[parity], :, c * chunk:(c + 1) * chunk]
        if bounded:
            pv = _dot(values, jnp.exp2(s).astype(_BF16))
            if state[i][parity] is not None:
                pv = state[i][parity][1] + pv
            mn = None
        else:
            mn = jnp.max(s, axis=0, keepdims=True)
            if state[i][parity] is not None:
                m, acc = state[i][parity]
                mn = jnp.maximum(m, mn)
            pv = _dot(values, jnp.exp2((s - mn).astype(_BF16)))
            if state[i][parity] is not None:
                pv = acc * jnp.exp2(m - mn) + pv
        state[i][parity] = (mn, pv)
        if c == n_chunks[i] - 1 and parity == 1:
            out_t = [acc[0:HEAD_DIM] * (1.0 / acc[HEAD_DIM:HEAD_DIM + 1]) for _, acc in state[i]]
            sink(jnp.concatenate(out_t, axis=0).T)

    for what, unit in order:
        if what == "scores":
            scores[unit] = issue_scores(*unit)
        else:
            finish(*unit, scores.pop(unit))


def _pool_delta(u_ref, up_ref, un_ref, inv_ref):
    tq = u_ref.shape[1]
    n_ext = tq + 2 * POOL_HALO
    is_first = pl.program_id(1) == 0
    is_last = pl.program_id(1) == pl.num_programs(1) - 1
    left = _is_even_head_lane((tq, LANES))
    deltas = []
    for half in range(POOL_W // LANES):
        w_a, w_b = POOL_WINDOWS[2 * half], POOL_WINDOWS[2 * half + 1]
        cols = slice(half * LANES, (half + 1) * LANES)
        u = u_ref[0, :, cols]
        ext = jnp.concatenate([jnp.where(is_first, 0.0, up_ref[0, :, cols]), u,
                               jnp.where(is_last, 0.0, un_ref[0, :, cols])], axis=0)
        sums = {}
        run = ext
        for w in POOL_WINDOWS[:2 * half + 2]:
            run = run + pltpu.roll(run, n_ext - w // 2, 0)
            if w in (w_a, w_b):
                aligned = run if w // 2 == POOL_HALO else pltpu.roll(run, w // 2, 0)
                first = 0 if w // 2 == POOL_HALO else POOL_HALO
                sums[w] = aligned[first:first + tq]
        mean = jnp.where(left, sums[w_a], sums[w_b]) * inv_ref[:, cols]
        deltas.append(mean - u)
    return jnp.concatenate(deltas, axis=1)


def _mix_kernel(qt_ref, qxt_ref, k_ref, vt_ref, mem_ref, gmem_ref, wmem_ref, gate_ref, u_ref, up_ref,
                un_ref, inv_ref, x_ref, wpool_ref, pscale_ref, wout_ref, gpost_ref, y_ref,
                mix_ref, km_ref, vmt_ref, *, bounded_scores):
    @pl.when(pl.program_id(1) == 0)
    def _():
        _memory_kv(mem_ref, gmem_ref, wmem_ref, km_ref, vmt_ref)

    d = _pool_delta(u_ref, up_ref, un_ref, inv_ref).astype(_BF16)
    pool = _dot(d, wpool_ref[...]) * pscale_ref[...]
    mix_ref[:, 0:POOL_W] = (pool * gate_ref[0, :, 0:POOL_W].astype(_F32)).astype(_BF16)

    tq = qt_ref.shape[3]
    pairs_per_group = N_HEADS // N_KV_HEADS // 2

    def gated_sink(r0, rows, first_cols):
        def sink(out):
            for n, c0 in enumerate(first_cols):
                gate = gate_ref[0, r0:r0 + rows, c0:c0 + LANES].astype(_F32)
                mix_ref[r0:r0 + rows, c0:c0 + LANES] = (out[n * rows:(n + 1) * rows] * gate).astype(_BF16)
        return sink

    streams = []
    for r0 in range(0, tq, X_SUB):
        for pair in range(N_XHEADS // 2):
            streams.append((qxt_ref[0, pair, :, r0:r0 + X_SUB], km_ref, 2 * pair,
                            vmt_ref, (2 * pair, 2 * pair + 1), N_MEM,
                            gated_sink(r0, X_SUB, [POOL_W + ATTN_W + pair * LANES]), False))
    for r0 in range(0, tq, Q_SUB):
        cols = slice(r0, r0 + Q_SUB)
        for first in range(0, N_HEADS // 2, PAIRS_PER_STREAM):
            group = first // pairs_per_group
            pairs = list(range(first, first + PAIRS_PER_STREAM))
            q_t = jnp.concatenate([qt_ref[0, pair, :, cols] for pair in pairs], axis=1)
            streams.append((q_t, k_ref.at[0], 2 * group, vt_ref.at[0], (group, group), KEY_CHUNK,
                            gated_sink(r0, Q_SUB, [POOL_W + pair * LANES for pair in pairs]),
                            bounded_scores))

    _attend(streams, ATTN_LOOKAHEAD)

    y = _dot(mix_ref[...], wout_ref[...])
    ms = jnp.mean(y * y, axis=-1, keepdims=True)
    y_ref[0] = x_ref[0] + y * lax.rsqrt(ms + EPS) * gpost_ref[...]


def _query_rows(t):
    f32, bf16, buffers = 4, 2, 2
    resident = buffers * bf16 * t * (2 * N_KV_HEADS * KV_W + N_KV_HEADS * VT_ROWS)
    weights = buffers * (bf16 * (MIX_W * D_MODEL + D_MODEL * 2 * XATTN_W + POOL_W * POOL_W)
                         + f32 * N_MEM * D_MODEL)
    per_row = (buffers * (2 * f32 * D_MODEL + bf16 * (MIX_W + ATTN_W + XATTN_W) + 2 * f32 * POOL_W)
               + bf16 * MIX_W)
    for rows in Q_ROWS_CANDIDATES:
        if t % rows == 0 and resident + weights + rows * per_row + MIX_TEMP_BYTES <= VMEM_LIMIT_BYTES:
            return rows
    raise ValueError(f"no query block size fits VMEM for sequence length {t}")


def _mix(layer, bounded_scores, qt, qxt, k, vt, mem, gmem, wmem, gate, u, pool_inv, x, wpool, pscale,
         wout, gpost):
    b, t, _ = x.shape
    tq = _query_rows(t)
    assert t % tq == 0 and tq % Q_SUB == 0 and tq % X_SUB == 0 and t % KEY_CHUNK == 0, (t, tq)
    n = t // tq
    halo_per_q = tq // POOL_HALO
    n_halo = t // POOL_HALO
    return pl.pallas_call(
        functools.partial(_mix_kernel, bounded_scores=bounded_scores),
        grid=(b, n),
        in_specs=[
            pl.BlockSpec((1, N_HEADS // 2, LANES, tq), lambda i, j: (i, 0, 0, j)),
            pl.BlockSpec((1, N_XHEADS // 2, LANES, tq), lambda i, j: (i, 0, 0, j)),
            pl.BlockSpec((1, 2 * N_KV_HEADS, t, KV_W), lambda i, j: (i, 0, 0, 0)),
            pl.BlockSpec((1, N_KV_HEADS, VT_ROWS, t), lambda i, j: (i, 0, 0, 0)),
            pl.BlockSpec((1, N_MEM, D_MODEL), lambda i, j: (i, 0, 0)),
            pl.BlockSpec((None, 1, D_MODEL), lambda i, j: (layer, 0, 0)),
            pl.BlockSpec((None, D_MODEL, 2 * XATTN_W), lambda i, j: (layer, 0, 0)),
            pl.BlockSpec((1, tq, MIX_W), lambda i, j: (i, j, 0)),
            pl.BlockSpec((1, tq, POOL_W), lambda i, j: (i, j, 0)),
            pl.BlockSpec((1, POOL_HALO, POOL_W),
                         lambda i, j: (i, jnp.maximum(j * halo_per_q - 1, 0), 0)),
            pl.BlockSpec((1, POOL_HALO, POOL_W),
                         lambda i, j: (i, jnp.minimum((j + 1) * halo_per_q, n_halo - 1), 0)),
            pl.BlockSpec((tq, POOL_W), lambda i, j: (j, 0)),
            pl.BlockSpec((1, tq, D_MODEL), lambda i, j: (i, j, 0)),
            pl.BlockSpec((None, POOL_W, POOL_W), lambda i, j: (layer, 0, 0)),
            pl.BlockSpec((None, 1, POOL_W), lambda i, j: (layer, 0, 0)),
            pl.BlockSpec((None, MIX_W, D_MODEL), lambda i, j: (layer, 0, 0)),
            pl.BlockSpec((None, 1, D_MODEL), lambda i, j: (layer, 0, 0)),
        ],
        out_specs=pl.BlockSpec((1, tq, D_MODEL), lambda i, j: (i, j, 0)),
        out_shape=jax.ShapeDtypeStruct((b, t, D_MODEL), _F32),
        scratch_shapes=[
            pltpu.VMEM((tq, MIX_W), _BF16),
            pltpu.VMEM((N_XHEADS, N_MEM, LANES), _BF16),
            pltpu.VMEM((N_XHEADS, VT_ROWS, N_MEM), _BF16),
        ],
        compiler_params=pltpu.CompilerParams(
            dimension_semantics=("arbitrary", "arbitrary"), vmem_limit_bytes=VMEM_LIMIT_BYTES),
        name="mix",
    )(qt, qxt, k, vt, mem, gmem, wmem, gate, u, u, u, pool_inv, x, wpool, pscale, wout, gpost)


def _rope_base(t):
    rows = t // GRID_W
    row = jnp.repeat(jnp.arange(rows), GRID_W).astype(_F32)
    col = jnp.tile(jnp.arange(GRID_W), rows).astype(_F32)
    freqs = ROPE_THETA ** (-jnp.arange(ROPE_PAIRS, dtype=_F32) / ROPE_PAIRS)
    ang = jnp.stack([row[:, None] * freqs, col[:, None] * freqs], axis=1)
    cos, sin = jnp.cos(ang), jnp.sin(ang)
    cos_h = jnp.stack([cos, cos], axis=2).reshape(t, HEAD_DIM)
    sin_h = jnp.stack([-sin, sin], axis=2).reshape(t, HEAD_DIM)
    return cos_h, sin_h


def _rope_tables(base, gains, scale):
    cos_h, sin_h = base
    g = gains.astype(_F32)
    g_sw = g.reshape(-1, 2, 2, ROPE_PAIRS)[:, :, ::-1, :].reshape(-1, HEAD_DIM)
    cos_t = cos_h[None] * (g * scale)[:, None, :]
    sin_t = sin_h[None] * (g_sw * scale)[:, None, :]
    reps = (1, 1, LANES // HEAD_DIM)
    return jnp.tile(cos_t, reps), jnp.tile(sin_t, reps)


def _pool_inv_counts(t):
    pos = jnp.arange(t)[:, None]
    w = jnp.repeat(jnp.array(POOL_WINDOWS), POOL_GROUP)[None, :]
    cnt = jnp.minimum(pos + (w - w // 2), t) - jnp.maximum(pos - w // 2, 0)
    return 1.0 / cnt.astype(_F32)


def _trunk(x, mem, prep):
    pool_inv = _pool_inv_counts(x.shape[1])
    for layer in range(DEPTH):
        u, gate, qt, qxt, k, vt = _proj(layer, x, prep["norm_pre"], prep["w_in"], prep["seg"],
                                        *prep["q_tables"], *prep["k_tables"])
        operands = (qt, qxt, k, vt, mem, prep["mem_norm"], prep["w_mem_kv"], gate, u, pool_inv, x,
                    prep["w_pool"], prep["pool_scale"], prep["w_out"], prep["norm_post"])
        x = lax.cond(prep["score_bound"][layer] <= BOUNDED_SCORE_LIMIT,
                     functools.partial(_mix, layer, True), functools.partial(_mix, layer, False),
                     *operands)
    return x


def kernel(x_prompt, x_sample, mem_prompt, mem_sample, norm_pre, norm_post, w_in, pool_w, pool_scale,
           q_norm, k_norm, mem_norm, w_mem_kv, w_out):
    n_win = len(POOL_WINDOWS)
    eye = jnp.eye(n_win, dtype=pool_w.dtype)
    base = _rope_base(max(x_prompt.shape[1], x_sample.shape[1]))
    prep = {
        "norm_pre": norm_pre[:, None, :], "norm_post": norm_post[:, None, :],
        "mem_norm": mem_norm[:, None, :], "pool_scale": pool_scale[:, None, :],
        "w_in": w_in.astype(_BF16), "w_out": w_out.astype(_BF16), "w_mem_kv": w_mem_kv.astype(_BF16),
        "w_pool": jnp.einsum("gh,lgcd->lgchd", eye, pool_w).reshape(DEPTH, POOL_W, POOL_W).astype(_BF16),
        "seg": jnp.kron(jnp.eye(MXU_WIDTH // HEAD_DIM, dtype=_F32),
                        jnp.ones((HEAD_DIM, HEAD_DIM), _F32)).astype(_BF16),
        "q_tables": _rope_tables(base, q_norm, Q_SCALE),
        "k_tables": _rope_tables(base, k_norm, 1.0),
        "score_bound": (Q_SCALE * HEAD_DIM * BF16_NORM_MARGIN
                        * jnp.max(jnp.abs(q_norm), axis=-1) * jnp.max(jnp.abs(k_norm), axis=-1)),
    }
    return (_trunk(x_prompt, mem_prompt, prep), _trunk(x_sample, mem_sample, prep))
```

```python
import functools
import math

import jax
import jax.numpy as jnp
from jax import lax
from jax.experimental import pallas as pl
from jax.experimental.pallas import tpu as pltpu

D_MODEL = 1024
DEPTH = 2
GRID_W = 64
N_MEM = 256
HEAD_DIM = 64
N_HEADS = 8
N_KV_HEADS = 2
ATTN_W = N_HEADS * HEAD_DIM
KV_W = N_KV_HEADS * HEAD_DIM
POOL_W = 256
POOL_WINDOWS = (2, 4, 8, 16)
POOL_GROUP = POOL_W // len(POOL_WINDOWS)
N_XHEADS = 4
XATTN_W = N_XHEADS * HEAD_DIM
MIX_W = POOL_W + ATTN_W + XATTN_W
IN_W = 2 * POOL_W + 2 * ATTN_W + 2 * KV_W + 2 * XATTN_W
ROPE_THETA = 10000.0
ROPE_PAIRS = HEAD_DIM // 4
EPS = 1e-6
Q_SCALE = math.log2(math.e) * HEAD_DIM ** -0.5
BOUNDED_SCORE_LIMIT = 40.0
BF16_NORM_MARGIN = 1.02

LANES = 128
MXU_WIDTH = 256
BF16_SUBLANES = 16
POOL_HALO = 8
PROJ_ROWS = 1024
PROJ_SUB = 512
KEY_CHUNK = 256
Q_ROWS_CANDIDATES = (1024, 512)
X_SUB = 512
MIX_TEMP_BYTES = 12 * 1024 * 1024
Q_SUB = 256
PAIRS_PER_STREAM = 2
ATTN_LOOKAHEAD = 2
VMEM_LIMIT_BYTES = 56 * 1024 * 1024

VT_ROWS = HEAD_DIM + BF16_SUBLANES

_OFF_U = 0
_OFF_GP = POOL_W
_OFF_Q = 2 * POOL_W
_OFF_K = _OFF_Q + ATTN_W
_OFF_V = _OFF_K + KV_W
_OFF_GA = _OFF_V + KV_W
_OFF_QX = _OFF_GA + ATTN_W
_OFF_GX = _OFF_QX + XATTN_W

_BF16 = jnp.bfloat16
_F32 = jnp.float32


def _dot(a, b):
    return jnp.dot(a, b, preferred_element_type=_F32)


def _silu(g):
    half = 0.5 * g
    return half + half * jnp.tanh(half)


def _is_even_head_lane(shape):
    return (lax.broadcasted_iota(jnp.int32, shape, len(shape) - 1) % LANES) < HEAD_DIM


def _ones_row_tile(cols):
    row = lax.broadcasted_iota(jnp.int32, (BF16_SUBLANES, cols), 0)
    return jnp.where(row == 0, 1.0, 0.0).astype(_BF16)


def _memory_kv(mem_ref, g_ref, w_ref, k_ref, vt_ref):
    m = mem_ref[0]
    ms = jnp.mean(m * m, axis=-1, keepdims=True)
    mh = (m * lax.rsqrt(ms + EPS) * g_ref[...]).astype(_BF16)
    kv = _dot(mh, w_ref[...])
    k = kv[:, :XATTN_W]
    v_t = kv[:, XATTN_W:].T
    left = _is_even_head_lane((N_MEM, LANES))
    for pair in range(N_XHEADS // 2):
        kp = k[:, pair * LANES:(pair + 1) * LANES]
        k_ref[2 * pair] = jnp.where(left, kp, 0.0).astype(_BF16)
        k_ref[2 * pair + 1] = jnp.where(left, 0.0, kp).astype(_BF16)
    for head in range(N_XHEADS):
        vt_ref[head, 0:HEAD_DIM] = v_t[head * HEAD_DIM:(head + 1) * HEAD_DIM].astype(_BF16)
        vt_ref[head, HEAD_DIM:VT_ROWS] = _ones_row_tile(N_MEM)


def _swap16(x):
    n = x.shape[-1]
    lane = lax.broadcasted_iota(jnp.int32, x.shape, 1)
    first = (lane & ROPE_PAIRS) == 0
    return jnp.where(first, pltpu.roll(x, n - ROPE_PAIRS, 1), pltpu.roll(x, ROPE_PAIRS, 1))


def _norm_rope(z, seg, cos_tab, sin_tab):
    reps = z.shape[1] // LANES
    cos_tab = jnp.concatenate([cos_tab] * reps, axis=1)
    sin_tab = jnp.concatenate([sin_tab] * reps, axis=1)
    ssq = _dot((z * z).astype(_BF16), seg)
    r = lax.rsqrt(ssq * (1.0 / HEAD_DIM) + EPS)
    return r * (z * cos_tab + _swap16(z) * sin_tab)


def _proj_kernel(x_ref, g_ref, w_ref, seg_ref, cq_ref, sq_ref, ck_ref, sk_ref,
                 u_ref, gate_ref, qt_ref, qxt_ref, k_ref, vt_ref):
    seg = seg_ref[...]
    seg_w = seg.shape[0]
    for r0 in range(0, x_ref.shape[1], PROJ_SUB):
        rs = slice(r0, r0 + PROJ_SUB)
        x = x_ref[0, rs, :]
        ms = jnp.mean(x * x, axis=-1, keepdims=True)
        h = (x * lax.rsqrt(ms + EPS) * g_ref[...]).astype(_BF16)

        z_qkv = _dot(h, w_ref[:, _OFF_Q:_OFF_GA])
        z_pool = _dot(h, w_ref[:, _OFF_U:_OFF_Q])

        cq, sq = cq_ref[rs, :], sq_ref[rs, :]
        for lo in range(0, ATTN_W, seg_w):
            q = _norm_rope(z_qkv[:, lo:lo + seg_w], seg, cq, sq)
            for pair in range(lo // LANES, (lo + seg_w) // LANES):
                qt_ref[0, pair, :, rs] = q[:, pair * LANES - lo:(pair + 1) * LANES - lo].T.astype(_BF16)

        k = _norm_rope(z_qkv[:, ATTN_W:ATTN_W + KV_W], seg[:KV_W, :KV_W],
                       ck_ref[rs, :], sk_ref[rs, :])
        k_sw = pltpu.roll(k, HEAD_DIM, 1)
        left = _is_even_head_lane(k.shape)
        k_ref[0, 0, rs, :] = jnp.where(left, k, 0.0).astype(_BF16)
        k_ref[0, 1, rs, :] = jnp.where(left, 0.0, k_sw).astype(_BF16)
        k_ref[0, 2, rs, :] = jnp.where(left, k_sw, 0.0).astype(_BF16)
        k_ref[0, 3, rs, :] = jnp.where(left, 0.0, k).astype(_BF16)

        v_t = z_qkv[:, ATTN_W + KV_W:].T
        for head in range(N_KV_HEADS):
            vt_ref[0, head, 0:HEAD_DIM, rs] = v_t[head * HEAD_DIM:(head + 1) * HEAD_DIM].astype(_BF16)
            vt_ref[0, head, HEAD_DIM:VT_ROWS, rs] = _ones_row_tile(PROJ_SUB)

        u_ref[0, rs, :] = z_pool[:, 0:POOL_W]
        gate_ref[0, rs, 0:POOL_W] = _silu(z_pool[:, POOL_W:]).astype(_BF16)

        z_rest = _dot(h, w_ref[:, _OFF_GA:])
        gate_ref[0, rs, POOL_W:POOL_W + ATTN_W] = _silu(z_rest[:, 0:ATTN_W]).astype(_BF16)
        gate_ref[0, rs, POOL_W + ATTN_W:MIX_W] = _silu(z_rest[:, ATTN_W + XATTN_W:]).astype(_BF16)
        for pair in range(N_XHEADS // 2):
            lo = ATTN_W + pair * LANES
            qxt_ref[0, pair, :, rs] = (z_rest[:, lo:lo + LANES] * Q_SCALE).T.astype(_BF16)


def _proj(layer, x, g, w, seg, cq, sq, ck, sk):
    b, t, _ = x.shape
    rows = PROJ_ROWS
    assert t % rows == 0 and rows % PROJ_SUB == 0 and t % GRID_W == 0, (t, rows)
    n = t // rows
    tab = pl.BlockSpec((None, rows, LANES), lambda i, j: (layer, j, 0))
    return pl.pallas_call(
        _proj_kernel,
        grid=(b, n),
        in_specs=[
            pl.BlockSpec((1, rows, D_MODEL), lambda i, j: (i, j, 0)),
            pl.BlockSpec((None, 1, D_MODEL), lambda i, j: (layer, 0, 0)),
            pl.BlockSpec((None, D_MODEL, IN_W), lambda i, j: (layer, 0, 0)),
            pl.BlockSpec((MXU_WIDTH, MXU_WIDTH), lambda i, j: (0, 0)),
            tab, tab, tab, tab,
        ],
        out_specs=[
            pl.BlockSpec((1, rows, POOL_W), lambda i, j: (i, j, 0)),
            pl.BlockSpec((1, rows, MIX_W), lambda i, j: (i, j, 0)),
            pl.BlockSpec((1, N_HEADS // 2, LANES, rows), lambda i, j: (i, 0, 0, j)),
            pl.BlockSpec((1, N_XHEADS // 2, LANES, rows), lambda i, j: (i, 0, 0, j)),
            pl.BlockSpec((1, 2 * N_KV_HEADS, rows, KV_W), lambda i, j: (i, 0, j, 0)),
            pl.BlockSpec((1, N_KV_HEADS, VT_ROWS, rows), lambda i, j: (i, 0, 0, j)),
        ],
        out_shape=[
            jax.ShapeDtypeStruct((b, t, POOL_W), _F32),
            jax.ShapeDtypeStruct((b, t, MIX_W), _BF16),
            jax.ShapeDtypeStruct((b, N_HEADS // 2, LANES, t), _BF16),
            jax.ShapeDtypeStruct((b, N_XHEADS // 2, LANES, t), _BF16),
            jax.ShapeDtypeStruct((b, 2 * N_KV_HEADS, t, KV_W), _BF16),
            jax.ShapeDtypeStruct((b, N_KV_HEADS, VT_ROWS, t), _BF16),
        ],
        compiler_params=pltpu.CompilerParams(
            dimension_semantics=("arbitrary", "arbitrary"), vmem_limit_bytes=VMEM_LIMIT_BYTES),
        name="proj",
    )(x, g, w, seg, cq, sq, ck, sk)


def _attend(streams, lookahead):
    n_chunks = [s[1].shape[1] // s[5] for s in streams]
    units = [(c, i, parity) for c in range(max(n_chunks)) for i in range(len(streams))
             if c < n_chunks[i] for parity in range(2)]
    units = [u for u in units if n_chunks[u[1]] > 1] + [u for u in units if n_chunks[u[1]] == 1]
    order = []
    for n, unit in enumerate(units):
        order.append(("scores", unit))
        if n >= lookahead:
            order.append(("finish", units[n - lookahead]))
    order += [("finish", unit) for unit in units[max(len(units) - lookahead, 0):]]
    state = [[None, None] for _ in streams]
    scores = {}

    def issue_scores(c, i, parity):
        q_t, k_ref, k_slot = streams[i][:3]
        chunk = streams[i][5]
        return _dot(k_ref[k_slot + parity, c * chunk:(c + 1) * chunk, :], q_t)

    def finish(c, i, parity, s):
        _, _, _, vt_ref, v_slots, chunk, sink, bounded = streams[i]
        values = vt_ref[v_slots[parity], :, c * chunk:(c + 1) * chunk]
        if bounded:
            pv = _dot(values, jnp.exp2(s).astype(_BF16))
            if state[i][parity] is not None:
                pv = state[i][parity][1] + pv
            mn = None
        else:
            mn = jnp.max(s, axis=0, keepdims=True)
            if state[i][parity] is not None:
                m, acc = state[i][parity]
                mn = jnp.maximum(m, mn)
            pv = _dot(values, jnp.exp2((s - mn).astype(_BF16)))
            if state[i][parity] is not None:
                pv = acc * jnp.exp2(m - mn) + pv
        state[i][parity] = (mn, pv)
        if c == n_chunks[i] - 1 and parity == 1:
            out_t = [acc[0:HEAD_DIM] * (1.0 / acc[HEAD_DIM:HEAD_DIM + 1]) for _, acc in state[i]]
            sink(jnp.concatenate(out_t, axis=0).T)

    for what, unit in order:
        if what == "scores":
            scores[unit] = issue_scores(*unit)
        else:
            finish(*unit, scores.pop(unit))


def _pool_delta(u_ref, up_ref, un_ref, inv_ref):
    tq = u_ref.shape[1]
    n_ext = tq + 2 * POOL_HALO
    is_first = pl.program_id(1) == 0
    is_last = pl.program_id(1) == pl.num_programs(1) - 1
    left = _is_even_head_lane((tq, LANES))
    deltas = []
    for half in range(POOL_W // LANES):
        w_a, w_b = POOL_WINDOWS[2 * half], POOL_WINDOWS[2 * half + 1]
        cols = slice(half * LANES, (half + 1) * LANES)
        u = u_ref[0, :, cols]
        ext = jnp.concatenate([jnp.where(is_first, 0.0, up_ref[0, :, cols]), u,
                               jnp.where(is_last, 0.0, un_ref[0, :, cols])], axis=0)
        sums = {}
        run = ext
        for w in POOL_WINDOWS[:2 * half + 2]:
            run = run + pltpu.roll(run, n_ext - w // 2, 0)
            if w in (w_a, w_b):
                aligned = run if w // 2 == POOL_HALO else pltpu.roll(run, w // 2, 0)
                first = 0 if w // 2 == POOL_HALO else POOL_HALO
                sums[w] = aligned[first:first + tq]
        mean = jnp.where(left, sums[w_a], sums[w_b]) * inv_ref[:, cols]
        deltas.append(mean - u)
    return jnp.concatenate(deltas, axis=1)


def _mix_kernel(qt_ref, qxt_ref, k_ref, vt_ref, mem_ref, gmem_ref, wmem_ref, gate_ref, u_ref, up_ref,
                un_ref, inv_ref, x_ref, wpool_ref, pscale_ref, wout_ref, gpost_ref, y_ref,
                mix_ref, km_ref, vmt_ref, *, bounded_scores):
    @pl.when(pl.program_id(1) == 0)
    def _():
        _memory_kv(mem_ref, gmem_ref, wmem_ref, km_ref, vmt_ref)

    d = _pool_delta(u_ref, up_ref, un_ref, inv_ref).astype(_BF16)
    pool = _dot(d, wpool_ref[...]) * pscale_ref[...]
    mix_ref[:, 0:POOL_W] = (pool * gate_ref[0, :, 0:POOL_W].astype(_F32)).astype(_BF16)

    tq = qt_ref.shape[3]
    pairs_per_group = N_HEADS // N_KV_HEADS // 2

    def gated_sink(r0, rows, first_cols):
        def sink(out):
            for n, c0 in enumerate(first_cols):
                gate = gate_ref[0, r0:r0 + rows, c0:c0 + LANES].astype(_F32)
                mix_ref[r0:r0 + rows, c0:c0 + LANES] = (out[n * rows:(n + 1) * rows] * gate).astype(_BF16)
        return sink

    streams = []
    for r0 in range(0, tq, X_SUB):
        for pair in range(N_XHEADS // 2):
            streams.append((qxt_ref[0, pair, :, r0:r0 + X_SUB], km_ref, 2 * pair,
                            vmt_ref, (2 * pair, 2 * pair + 1), N_MEM,
                            gated_sink(r0, X_SUB, [POOL_W + ATTN_W + pair * LANES]), False))
    for r0 in range(0, tq, Q_SUB):
        cols = slice(r0, r0 + Q_SUB)
        for first in range(0, N_HEADS // 2, PAIRS_PER_STREAM):
            group = first // pairs_per_group
            pairs = list(range(first, first + PAIRS_PER_STREAM))
            q_t = jnp.concatenate([qt_ref[0, pair, :, cols] for pair in pairs], axis=1)
            streams.append((q_t, k_ref.at[0], 2 * group, vt_ref.at[0], (group, group), KEY_CHUNK,
                            gated_sink(r0, Q_SUB, [POOL_W + pair * LANES for pair in pairs]),
                            bounded_scores))

    _attend(streams, ATTN_LOOKAHEAD)

    y = _dot(mix_ref[...], wout_ref[...])
    ms = jnp.mean(y * y, axis=-1, keepdims=True)
    y_ref[0] = x_ref[0] + y * lax.rsqrt(ms + EPS) * gpost_ref[...]


def _query_rows(t):
    f32, bf16, buffers = 4, 2, 2
    resident = buffers * bf16 * t * (2 * N_KV_HEADS * KV_W + N_KV_HEADS * VT_ROWS)
    weights = buffers * (bf16 * (MIX_W * D_MODEL + D_MODEL * 2 * XATTN_W + POOL_W * POOL_W)
                         + f32 * N_MEM * D_MODEL)
    per_row = (buffers * (2 * f32 * D_MODEL + bf16 * (MIX_W + ATTN_W + XATTN_W) + 2 * f32 * POOL_W)
               + bf16 * MIX_W)
    for rows in Q_ROWS_CANDIDATES:
        if t % rows == 0 and resident + weights + rows * per_row + MIX_TEMP_BYTES <= VMEM_LIMIT_BYTES:
            return rows
    raise ValueError(f"no query block size fits VMEM for sequence length {t}")


def _mix(layer, bounded_scores, qt, qxt, k, vt, mem, gmem, wmem, gate, u, pool_inv, x, wpool, pscale,
         wout, gpost):
    b, t, _ = x.shape
    tq = _query_rows(t)
    assert t % tq == 0 and tq % Q_SUB == 0 and tq % X_SUB == 0 and t % KEY_CHUNK == 0, (t, tq)
    n = t // tq
    halo_per_q = tq // POOL_HALO
    n_halo = t // POOL_HALO
    return pl.pallas_call(
        functools.partial(_mix_kernel, bounded_scores=bounded_scores),
        grid=(b, n),
        in_specs=[
            pl.BlockSpec((1, N_HEADS // 2, LANES, tq), lambda i, j: (i, 0, 0, j)),
            pl.BlockSpec((1, N_XHEADS // 2, LANES, tq), lambda i, j: (i, 0, 0, j)),
            pl.BlockSpec((1, 2 * N_KV_HEADS, t, KV_W), lambda i, j: (i, 0, 0, 0)),
            pl.BlockSpec((1, N_KV_HEADS, VT_ROWS, t), lambda i, j: (i, 0, 0, 0)),
            pl.BlockSpec((1, N_MEM, D_MODEL), lambda i, j: (i, 0, 0)),
            pl.BlockSpec((None, 1, D_MODEL), lambda i, j: (layer, 0, 0)),
            pl.BlockSpec((None, D_MODEL, 2 * XATTN_W), lambda i, j: (layer, 0, 0)),
            pl.BlockSpec((1, tq, MIX_W), lambda i, j: (i, j, 0)),
            pl.BlockSpec((1, tq, POOL_W), lambda i, j: (i, j, 0)),
            pl.BlockSpec((1, POOL_HALO, POOL_W),
                         lambda i, j: (i, jnp.maximum(j * halo_per_q - 1, 0), 0)),
            pl.BlockSpec((1, POOL_HALO, POOL_W),
                         lambda i, j: (i, jnp.minimum((j + 1) * halo_per_q, n_halo - 1), 0)),
            pl.BlockSpec((tq, POOL_W), lambda i, j: (j, 0)),
            pl.BlockSpec((1, tq, D_MODEL), lambda i, j: (i, j, 0)),
            pl.BlockSpec((None, POOL_W, POOL_W), lambda i, j: (layer, 0, 0)),
            pl.BlockSpec((None, 1, POOL_W), lambda i, j: (layer, 0, 0)),
            pl.BlockSpec((None, MIX_W, D_MODEL), lambda i, j: (layer, 0, 0)),
            pl.BlockSpec((None, 1, D_MODEL), lambda i, j: (layer, 0, 0)),
        ],
        out_specs=pl.BlockSpec((1, tq, D_MODEL), lambda i, j: (i, j, 0)),
        out_shape=jax.ShapeDtypeStruct((b, t, D_MODEL), _F32),
        scratch_shapes=[
            pltpu.VMEM((tq, MIX_W), _BF16),
            pltpu.VMEM((N_XHEADS, N_MEM, LANES), _BF16),
            pltpu.VMEM((N_XHEADS, VT_ROWS, N_MEM), _BF16),
        ],
        compiler_params=pltpu.CompilerParams(
            dimension_semantics=("arbitrary", "arbitrary"), vmem_limit_bytes=VMEM_LIMIT_BYTES),
        name="mix",
    )(qt, qxt, k, vt, mem, gmem, wmem, gate, u, u, u, pool_inv, x, wpool, pscale, wout, gpost)


def _rope_base(t):
    rows = t // GRID_W
    row = jnp.repeat(jnp.arange(rows), GRID_W).astype(_F32)
    col = jnp.tile(jnp.arange(GRID_W), rows).astype(_F32)
    freqs = ROPE_THETA ** (-jnp.arange(ROPE_PAIRS, dtype=_F32) / ROPE_PAIRS)
    ang = jnp.stack([row[:, None] * freqs, col[:, None] * freqs], axis=1)
    cos, sin = jnp.cos(ang), jnp.sin(ang)
    cos_h = jnp.stack([cos, cos], axis=2).reshape(t, HEAD_DIM)
    sin_h = jnp.stack([-sin, sin], axis=2).reshape(t, HEAD_DIM)
    return cos_h, sin_h


def _rope_tables(base, gains, scale):
    cos_h, sin_h = base
    g = gains.astype(_F32)
    g_sw = g.reshape(-1, 2, 2, ROPE_PAIRS)[:, :, ::-1, :].reshape(-1, HEAD_DIM)
    cos_t = cos_h[None] * (g * scale)[:, None, :]
    sin_t = sin_h[None] * (g_sw * scale)[:, None, :]
    reps = (1, 1, LANES // HEAD_DIM)
    return jnp.tile(cos_t, reps), jnp.tile(sin_t, reps)


def _pool_inv_counts(t):
    pos = jnp.arange(t)[:, None]
    w = jnp.repeat(jnp.array(POOL_WINDOWS), POOL_GROUP)[None, :]
    cnt = jnp.minimum(pos + (w - w // 2), t) - jnp.maximum(pos - w // 2, 0)
    return 1.0 / cnt.astype(_F32)


def _trunk(bounded_scores, x, mem, prep):
    pool_inv = _pool_inv_counts(x.shape[1])
    for layer in range(DEPTH):
        u, gate, qt, qxt, k, vt = _proj(layer, x, prep["norm_pre"], prep["w_in"], prep["seg"],
                                        *prep["q_tables"], *prep["k_tables"])
        x = _mix(layer, bounded_scores, qt, qxt, k, vt, mem, prep["mem_norm"], prep["w_mem_kv"], gate,
                 u, pool_inv, x, prep["w_pool"], prep["pool_scale"], prep["w_out"], prep["norm_post"])
    return x


def _both_trunks(bounded_scores, x_prompt, x_sample, mem_prompt, mem_sample, prep):
    return (_trunk(bounded_scores, x_prompt, mem_prompt, prep),
            _trunk(bounded_scores, x_sample, mem_sample, prep))


def kernel(x_prompt, x_sample, mem_prompt, mem_sample, norm_pre, norm_post, w_in, pool_w, pool_scale,
           q_norm, k_norm, mem_norm, w_mem_kv, w_out):
    n_win = len(POOL_WINDOWS)
    eye = jnp.eye(n_win, dtype=pool_w.dtype)
    base = _rope_base(max(x_prompt.shape[1], x_sample.shape[1]))
    prep = {
        "norm_pre": norm_pre[:, None, :], "norm_post": norm_post[:, None, :],
        "mem_norm": mem_norm[:, None, :], "pool_scale": pool_scale[:, None, :],
        "w_in": w_in.astype(_BF16), "w_out": w_out.astype(_BF16), "w_mem_kv": w_mem_kv.astype(_BF16),
        "w_pool": jnp.einsum("gh,lgcd->lgchd", eye, pool_w).reshape(DEPTH, POOL_W, POOL_W).astype(_BF16),
        "seg": jnp.kron(jnp.eye(MXU_WIDTH // HEAD_DIM, dtype=_F32),
                        jnp.ones((HEAD_DIM, HEAD_DIM), _F32)).astype(_BF16),
        "q_tables": _rope_tables(base, q_norm, Q_SCALE),
        "k_tables": _rope_tables(base, k_norm, 1.0),
    }
    score_bound = (Q_SCALE * HEAD_DIM * BF16_NORM_MARGIN
                   * jnp.max(jnp.abs(q_norm), axis=-1) * jnp.max(jnp.abs(k_norm), axis=-1))
    return lax.cond(jnp.all(score_bound <= BOUNDED_SCORE_LIMIT),
                    functools.partial(_both_trunks, True), functools.partial(_both_trunks, False),
                    x_prompt, x_sample, mem_prompt, mem_sample, prep)
```

```python
import functools
import math

import jax
import jax.numpy as jnp
from jax import lax
from jax.experimental import pallas as pl
from jax.experimental.pallas import tpu as pltpu

D_MODEL = 1024
DEPTH = 2
GRID_W = 64
N_MEM = 256
HEAD_DIM = 64
N_HEADS = 8
N_KV_HEADS = 2
ATTN_W = N_HEADS * HEAD_DIM
KV_W = N_KV_HEADS * HEAD_DIM
POOL_W = 256
POOL_WINDOWS = (2, 4, 8, 16)
POOL_GROUP = POOL_W // len(POOL_WINDOWS)
N_XHEADS = 4
XATTN_W = N_XHEADS * HEAD_DIM
MIX_W = POOL_W + ATTN_W + XATTN_W
IN_W = 2 * POOL_W + 2 * ATTN_W + 2 * KV_W + 2 * XATTN_W
ROPE_THETA = 10000.0
ROPE_PAIRS = HEAD_DIM // 4
EPS = 1e-6
Q_SCALE = math.log2(math.e) * HEAD_DIM ** -0.5
BOUNDED_SCORE_LIMIT = 40.0
BF16_NORM_MARGIN = 1.02

LANES = 128
MXU_WIDTH = 256
BF16_SUBLANES = 16
POOL_HALO = 8
PROJ_ROWS = 1024
PROJ_SUB = 512
KEY_CHUNK = 256
Q_ROWS_CANDIDATES = (1024, 512)
X_SUB = 512
MIX_TEMP_BYTES = 12 * 1024 * 1024
Q_SUB = 256
PAIRS_PER_STREAM = 2
ATTN_LOOKAHEAD = 2
VMEM_LIMIT_BYTES = 56 * 1024 * 1024

VT_ROWS = HEAD_DIM + BF16_SUBLANES

_OFF_U = 0
_OFF_GP = POOL_W
_OFF_Q = 2 * POOL_W
_OFF_K = _OFF_Q + ATTN_W
_OFF_V = _OFF_K + KV_W
_OFF_GA = _OFF_V + KV_W
_OFF_QX = _OFF_GA + ATTN_W
_OFF_GX = _OFF_QX + XATTN_W

_RESIDENT = pl.Buffered(1)

_BF16 = jnp.bfloat16
_F32 = jnp.float32


def _dot(a, b):
    return jnp.dot(a, b, preferred_element_type=_F32)


def _silu(g):
    half = 0.5 * g
    return half + half * jnp.tanh(half)


def _is_even_head_lane(shape):
    return (lax.broadcasted_iota(jnp.int32, shape, len(shape) - 1) % LANES) < HEAD_DIM


def _ones_row_tile(cols):
    row = lax.broadcasted_iota(jnp.int32, (BF16_SUBLANES, cols), 0)
    return jnp.where(row == 0, 1.0, 0.0).astype(_BF16)


def _memory_kv(mem_ref, g_ref, w_ref, k_ref, vt_ref):
    m = mem_ref[0]
    ms = jnp.mean(m * m, axis=-1, keepdims=True)
    mh = (m * lax.rsqrt(ms + EPS) * g_ref[...]).astype(_BF16)
    kv = _dot(mh, w_ref[...])
    k = kv[:, :XATTN_W]
    v_t = kv[:, XATTN_W:].T
    left = _is_even_head_lane((N_MEM, LANES))
    for pair in range(N_XHEADS // 2):
        kp = k[:, pair * LANES:(pair + 1) * LANES]
        k_ref[2 * pair] = jnp.where(left, kp, 0.0).astype(_BF16)
        k_ref[2 * pair + 1] = jnp.where(left, 0.0, kp).astype(_BF16)
    for head in range(N_XHEADS):
        vt_ref[head, 0:HEAD_DIM] = v_t[head * HEAD_DIM:(head + 1) * HEAD_DIM].astype(_BF16)
        vt_ref[head, HEAD_DIM:VT_ROWS] = _ones_row_tile(N_MEM)


def _swap16(x):
    n = x.shape[-1]
    lane = lax.broadcasted_iota(jnp.int32, x.shape, 1)
    first = (lane & ROPE_PAIRS) == 0
    return jnp.where(first, pltpu.roll(x, n - ROPE_PAIRS, 1), pltpu.roll(x, ROPE_PAIRS, 1))


def _norm_rope(z, seg, cos_tab, sin_tab):
    reps = z.shape[1] // LANES
    cos_tab = jnp.concatenate([cos_tab] * reps, axis=1)
    sin_tab = jnp.concatenate([sin_tab] * reps, axis=1)
    ssq = _dot((z * z).astype(_BF16), seg)
    r = lax.rsqrt(ssq * (1.0 / HEAD_DIM) + EPS)
    return r * (z * cos_tab + _swap16(z) * sin_tab)


def _proj_kernel(x_ref, g_ref, w_ref, seg_ref, cq_ref, sq_ref, ck_ref, sk_ref,
                 u_ref, gate_ref, qt_ref, qxt_ref, k_ref, vt_ref):
    seg = seg_ref[...]
    seg_w = seg.shape[0]
    for r0 in range(0, x_ref.shape[1], PROJ_SUB):
        rs = slice(r0, r0 + PROJ_SUB)
        x = x_ref[0, rs, :]
        ms = jnp.mean(x * x, axis=-1, keepdims=True)
        h = (x * lax.rsqrt(ms + EPS) * g_ref[...]).astype(_BF16)

        z_qkv = _dot(h, w_ref[:, _OFF_Q:_OFF_GA])
        z_pool = _dot(h, w_ref[:, _OFF_U:_OFF_Q])

        cq, sq = cq_ref[rs, :], sq_ref[rs, :]
        for lo in range(0, ATTN_W, seg_w):
            q = _norm_rope(z_qkv[:, lo:lo + seg_w], seg, cq, sq)
            for pair in range(lo // LANES, (lo + seg_w) // LANES):
                qt_ref[0, pair, :, rs] = q[:, pair * LANES - lo:(pair + 1) * LANES - lo].T.astype(_BF16)

        k = _norm_rope(z_qkv[:, ATTN_W:ATTN_W + KV_W], seg[:KV_W, :KV_W],
                       ck_ref[rs, :], sk_ref[rs, :])
        k_sw = pltpu.roll(k, HEAD_DIM, 1)
        left = _is_even_head_lane(k.shape)
        k_ref[0, 0, rs, :] = jnp.where(left, k, 0.0).astype(_BF16)
        k_ref[0, 1, rs, :] = jnp.where(left, 0.0, k_sw).astype(_BF16)
        k_ref[0, 2, rs, :] = jnp.where(left, k_sw, 0.0).astype(_BF16)
        k_ref[0, 3, rs, :] = jnp.where(left, 0.0, k).astype(_BF16)

        v_t = z_qkv[:, ATTN_W + KV_W:].T
        for head in range(N_KV_HEADS):
            vt_ref[0, head, 0:HEAD_DIM, rs] = v_t[head * HEAD_DIM:(head + 1) * HEAD_DIM].astype(_BF16)
            vt_ref[0, head, HEAD_DIM:VT_ROWS, rs] = _ones_row_tile(PROJ_SUB)

        u_ref[0, rs, :] = z_pool[:, 0:POOL_W]
        gate_ref[0, rs, 0:POOL_W] = _silu(z_pool[:, POOL_W:]).astype(_BF16)

        z_rest = _dot(h, w_ref[:, _OFF_GA:])
        gate_ref[0, rs, POOL_W:POOL_W + ATTN_W] = _silu(z_rest[:, 0:ATTN_W]).astype(_BF16)
        gate_ref[0, rs, POOL_W + ATTN_W:MIX_W] = _silu(z_rest[:, ATTN_W + XATTN_W:]).astype(_BF16)
        for pair in range(N_XHEADS // 2):
            lo = ATTN_W + pair * LANES
            qxt_ref[0, pair, :, rs] = (z_rest[:, lo:lo + LANES] * Q_SCALE).T.astype(_BF16)


def _proj(layer, x, g, w, seg, cq, sq, ck, sk):
    b, t, _ = x.shape
    rows = PROJ_ROWS
    assert t % rows == 0 and rows % PROJ_SUB == 0 and t % GRID_W == 0, (t, rows)
    n = t // rows
    tab = pl.BlockSpec((None, rows, LANES), lambda i, j: (layer, j, 0))
    return pl.pallas_call(
        _proj_kernel,
        grid=(b, n),
        in_specs=[
            pl.BlockSpec((1, rows, D_MODEL), lambda i, j: (i, j, 0)),
            pl.BlockSpec((None, 1, D_MODEL), lambda i, j: (layer, 0, 0)),
            pl.BlockSpec((None, D_MODEL, IN_W), lambda i, j: (layer, 0, 0), pipeline_mode=_RESIDENT),
            pl.BlockSpec((MXU_WIDTH, MXU_WIDTH), lambda i, j: (0, 0), pipeline_mode=_RESIDENT),
            tab, tab, tab, tab,
        ],
        out_specs=[
            pl.BlockSpec((1, rows, POOL_W), lambda i, j: (i, j, 0)),
            pl.BlockSpec((1, rows, MIX_W), lambda i, j: (i, j, 0)),
            pl.BlockSpec((1, N_HEADS // 2, LANES, rows), lambda i, j: (i, 0, 0, j)),
            pl.BlockSpec((1, N_XHEADS // 2, LANES, rows), lambda i, j: (i, 0, 0, j)),
            pl.BlockSpec((1, 2 * N_KV_HEADS, rows, KV_W), lambda i, j: (i, 0, j, 0)),
            pl.BlockSpec((1, N_KV_HEADS, VT_ROWS, rows), lambda i, j: (i, 0, 0, j)),
        ],
        out_shape=[
            jax.ShapeDtypeStruct((b, t, POOL_W), _F32),
            jax.ShapeDtypeStruct((b, t, MIX_W), _BF16),
            jax.ShapeDtypeStruct((b, N_HEADS // 2, LANES, t), _BF16),
            jax.ShapeDtypeStruct((b, N_XHEADS // 2, LANES, t), _BF16),
            jax.ShapeDtypeStruct((b, 2 * N_KV_HEADS, t, KV_W), _BF16),
            jax.ShapeDtypeStruct((b, N_KV_HEADS, VT_ROWS, t), _BF16),
        ],
        compiler_params=pltpu.CompilerParams(
            dimension_semantics=("arbitrary", "arbitrary"), vmem_limit_bytes=VMEM_LIMIT_BYTES),
        name="proj",
    )(x, g, w, seg, cq, sq, ck, sk)


def _attend(streams, lookahead):
    n_chunks = [s[1].shape[1] // s[5] for s in streams]
    units = [(c, i, parity) for c in range(max(n_chunks)) for i in range(len(streams))
             if c < n_chunks[i] for parity in range(2)]
    units = [u for u in units if n_chunks[u[1]] > 1] + [u for u in units if n_chunks[u[1]] == 1]
    order = []
    for n, unit in enumerate(units):
        order.append(("scores", unit))
        if n >= lookahead:
            order.append(("finish", units[n - lookahead]))
    order += [("finish", unit) for unit in units[max(len(units) - lookahead, 0):]]
    state = [[None, None] for _ in streams]
    scores = {}

    def issue_scores(c, i, parity):
        q_t, k_ref, k_slot = streams[i][:3]
        chunk = streams[i][5]
        return _dot(k_ref[k_slot + parity, c * chunk:(c + 1) * chunk, :], q_t)

    def finish(c, i, parity, s):
        _, _, _, vt_ref, v_slots, chunk, sink, bounded = streams[i]
        values = vt_ref[v_slots[parity], :, c * chunk:(c + 1) * chunk]
        if bounded:
            pv = _dot(values, jnp.exp2(s).astype(_BF16))
            if state[i][parity] is not None:
                pv = state[i][parity][1] + pv
            mn = None
        else:
            mn = jnp.max(s, axis=0, keepdims=True)
            if state[i][parity] is not None:
                m, acc = state[i][parity]
                mn = jnp.maximum(m, mn)
            pv = _dot(values, jnp.exp2((s - mn).astype(_BF16)))
            if state[i][parity] is not None:
                pv = acc * jnp.exp2(m - mn) + pv
        state[i][parity] = (mn, pv)
        if c == n_chunks[i] - 1 and parity == 1:
            out_t = [acc[0:HEAD_DIM] * (1.0 / acc[HEAD_DIM:HEAD_DIM + 1]) for _, acc in state[i]]
            sink(jnp.concatenate(out_t, axis=0).T)

    for what, unit in order:
        if what == "scores":
            scores[unit] = issue_scores(*unit)
        else:
            finish(*unit, scores.pop(unit))


def _pool_delta(u_ref, up_ref, un_ref, inv_ref):
    tq = u_ref.shape[1]
    n_ext = tq + 2 * POOL_HALO
    is_first = pl.program_id(1) == 0
    is_last = pl.program_id(1) == pl.num_programs(1) - 1
    left = _is_even_head_lane((tq, LANES))
    deltas = []
    for half in range(POOL_W // LANES):
        w_a, w_b = POOL_WINDOWS[2 * half], POOL_WINDOWS[2 * half + 1]
        cols = slice(half * LANES, (half + 1) * LANES)
        u = u_ref[0, :, cols]
        ext = jnp.concatenate([jnp.where(is_first, 0.0, up_ref[0, :, cols]), u,
                               jnp.where(is_last, 0.0, un_ref[0, :, cols])], axis=0)
        sums = {}
        run = ext
        for w in POOL_WINDOWS[:2 * half + 2]:
            run = run + pltpu.roll(run, n_ext - w // 2, 0)
            if w in (w_a, w_b):
                aligned = run if w // 2 == POOL_HALO else pltpu.roll(run, w // 2, 0)
                first = 0 if w // 2 == POOL_HALO else POOL_HALO
                sums[w] = aligned[first:first + tq]
        mean = jnp.where(left, sums[w_a], sums[w_b]) * inv_ref[:, cols]
        deltas.append(mean - u)
    return jnp.concatenate(deltas, axis=1)


def _mix_kernel(qt_ref, qxt_ref, k_ref, vt_ref, mem_ref, gmem_ref, wmem_ref, gate_ref, u_ref, up_ref,
                un_ref, inv_ref, x_ref, wpool_ref, pscale_ref, wout_ref, gpost_ref, y_ref,
                mix_ref, km_ref, vmt_ref, *, bounded_scores):
    @pl.when(pl.program_id(1) == 0)
    def _():
        _memory_kv(mem_ref, gmem_ref, wmem_ref, km_ref, vmt_ref)

    d = _pool_delta(u_ref, up_ref, un_ref, inv_ref).astype(_BF16)
    pool = _dot(d, wpool_ref[...]) * pscale_ref[...]
    mix_ref[:, 0:POOL_W] = (pool * gate_ref[0, :, 0:POOL_W].astype(_F32)).astype(_BF16)

    tq = qt_ref.shape[3]
    pairs_per_group = N_HEADS // N_KV_HEADS // 2

    def gated_sink(r0, rows, first_cols):
        def sink(out):
            for n, c0 in enumerate(first_cols):
                gate = gate_ref[0, r0:r0 + rows, c0:c0 + LANES].astype(_F32)
                mix_ref[r0:r0 + rows, c0:c0 + LANES] = (out[n * rows:(n + 1) * rows] * gate).astype(_BF16)
        return sink

    streams = []
    for r0 in range(0, tq, X_SUB):
        for pair in range(N_XHEADS // 2):
            streams.append((qxt_ref[0, pair, :, r0:r0 + X_SUB], km_ref, 2 * pair,
                            vmt_ref, (2 * pair, 2 * pair + 1), N_MEM,
                            gated_sink(r0, X_SUB, [POOL_W + ATTN_W + pair * LANES]), False))
    for r0 in range(0, tq, Q_SUB):
        cols = slice(r0, r0 + Q_SUB)
        for first in range(0, N_HEADS // 2, PAIRS_PER_STREAM):
            group = first // pairs_per_group
            pairs = list(range(first, first + PAIRS_PER_STREAM))
            q_t = jnp.concatenate([qt_ref[0, pair, :, cols] for pair in pairs], axis=1)
            streams.append((q_t, k_ref.at[0], 2 * group, vt_ref.at[0], (group, group), KEY_CHUNK,
                            gated_sink(r0, Q_SUB, [POOL_W + pair * LANES for pair in pairs]),
                            bounded_scores))

    _attend(streams, ATTN_LOOKAHEAD)

    y = _dot(mix_ref[...], wout_ref[...])
    ms = jnp.mean(y * y, axis=-1, keepdims=True)
    y_ref[0] = x_ref[0] + y * lax.rsqrt(ms + EPS) * gpost_ref[...]


def _query_rows(t):
    f32, bf16, buffers = 4, 2, 2
    resident = buffers * bf16 * t * (2 * N_KV_HEADS * KV_W + N_KV_HEADS * VT_ROWS)
    weights = buffers * (bf16 * (MIX_W * D_MODEL + D_MODEL * 2 * XATTN_W + POOL_W * POOL_W)
                         + f32 * N_MEM * D_MODEL)
    per_row = (buffers * (2 * f32 * D_MODEL + bf16 * (MIX_W + ATTN_W + XATTN_W) + 2 * f32 * POOL_W)
               + bf16 * MIX_W)
    for rows in Q_ROWS_CANDIDATES:
        if t % rows == 0 and resident + weights + rows * per_row + MIX_TEMP_BYTES <= VMEM_LIMIT_BYTES:
            return rows
    raise ValueError(f"no query block size fits VMEM for sequence length {t}")


def _mix(layer, bounded_scores, qt, qxt, k, vt, mem, gmem, wmem, gate, u, pool_inv, x, wpool, pscale,
         wout, gpost):
    b, t, _ = x.shape
    tq = _query_rows(t)
    assert t % tq == 0 and tq % Q_SUB == 0 and tq % X_SUB == 0 and t % KEY_CHUNK == 0, (t, tq)
    n = t // tq
    halo_per_q = tq // POOL_HALO
    n_halo = t // POOL_HALO
    return pl.pallas_call(
        functools.partial(_mix_kernel, bounded_scores=bounded_scores),
        grid=(b, n),
        in_specs=[
            pl.BlockSpec((1, N_HEADS // 2, LANES, tq), lambda i, j: (i, 0, 0, j)),
            pl.BlockSpec((1, N_XHEADS // 2, LANES, tq), lambda i, j: (i, 0, 0, j)),
            pl.BlockSpec((1, 2 * N_KV_HEADS, t, KV_W), lambda i, j: (i, 0, 0, 0)),
            pl.BlockSpec((1, N_KV_HEADS, VT_ROWS, t), lambda i, j: (i, 0, 0, 0)),
            pl.BlockSpec((1, N_MEM, D_MODEL), lambda i, j: (i, 0, 0)),
            pl.BlockSpec((None, 1, D_MODEL), lambda i, j: (layer, 0, 0)),
            pl.BlockSpec((None, D_MODEL, 2 * XATTN_W), lambda i, j: (layer, 0, 0), pipeline_mode=_RESIDENT),
            pl.BlockSpec((1, tq, MIX_W), lambda i, j: (i, j, 0)),
            pl.BlockSpec((1, tq, POOL_W), lambda i, j: (i, j, 0)),
            pl.BlockSpec((1, POOL_HALO, POOL_W),
                         lambda i, j: (i, jnp.maximum(j * halo_per_q - 1, 0), 0)),
            pl.BlockSpec((1, POOL_HALO, POOL_W),
                         lambda i, j: (i, jnp.minimum((j + 1) * halo_per_q, n_halo - 1), 0)),
            pl.BlockSpec((tq, POOL_W), lambda i, j: (j, 0)),
            pl.BlockSpec((1, tq, D_MODEL), lambda i, j: (i, j, 0)),
            pl.BlockSpec((None, POOL_W, POOL_W), lambda i, j: (layer, 0, 0), pipeline_mode=_RESIDENT),
            pl.BlockSpec((None, 1, POOL_W), lambda i, j: (layer, 0, 0)),
            pl.BlockSpec((None, MIX_W, D_MODEL), lambda i, j: (layer, 0, 0), pipeline_mode=_RESIDENT),
            pl.BlockSpec((None, 1, D_MODEL), lambda i, j: (layer, 0, 0)),
        ],
        out_specs=pl.BlockSpec((1, tq, D_MODEL), lambda i, j: (i, j, 0)),
        out_shape=jax.ShapeDtypeStruct((b, t, D_MODEL), _F32),
        scratch_shapes=[
            pltpu.VMEM((tq, MIX_W), _BF16),
            pltpu.VMEM((N_XHEADS, N_MEM, LANES), _BF16),
            pltpu.VMEM((N_XHEADS, VT_ROWS, N_MEM), _BF16),
        ],
        compiler_params=pltpu.CompilerParams(
            dimension_semantics=("arbitrary", "arbitrary"), vmem_limit_bytes=VMEM_LIMIT_BYTES),
        name="mix",
    )(qt, qxt, k, vt, mem, gmem, wmem, gate, u, u, u, pool_inv, x, wpool, pscale, wout, gpost)


def _rope_base(t):
    rows = t // GRID_W
    row = jnp.repeat(jnp.arange(rows), GRID_W).astype(_F32)
    col = jnp.tile(jnp.arange(GRID_W), rows).astype(_F32)
    freqs = ROPE_THETA ** (-jnp.arange(ROPE_PAIRS, dtype=_F32) / ROPE_PAIRS)
    ang = jnp.stack([row[:, None] * freqs, col[:, None] * freqs], axis=1)
    cos, sin = jnp.cos(ang), jnp.sin(ang)
    cos_h = jnp.stack([cos, cos], axis=2).reshape(t, HEAD_DIM)
    sin_h = jnp.stack([-sin, sin], axis=2).reshape(t, HEAD_DIM)
    return cos_h, sin_h


def _rope_tables(base, gains, scale):
    cos_h, sin_h = base
    g = gains.astype(_F32)
    g_sw = g.reshape(-1, 2, 2, ROPE_PAIRS)[:, :, ::-1, :].reshape(-1, HEAD_DIM)
    cos_t = cos_h[None] * (g * scale)[:, None, :]
    sin_t = sin_h[None] * (g_sw * scale)[:, None, :]
    reps = (1, 1, LANES // HEAD_DIM)
    return jnp.tile(cos_t, reps), jnp.tile(sin_t, reps)


def _pool_inv_counts(t):
    pos = jnp.arange(t)[:, None]
    w = jnp.repeat(jnp.array(POOL_WINDOWS), POOL_GROUP)[None, :]
    cnt = jnp.minimum(pos + (w - w // 2), t) - jnp.maximum(pos - w // 2, 0)
    return 1.0 / cnt.astype(_F32)


def _trunk(bounded_scores, x, mem, prep):
    pool_inv = _pool_inv_counts(x.shape[1])
    for layer in range(DEPTH):
        u, gate, qt, qxt, k, vt = _proj(layer, x, prep["norm_pre"], prep["w_in"], prep["seg"],
                                        *prep["q_tables"], *prep["k_tables"])
        x = _mix(layer, bounded_scores, qt, qxt, k, vt, mem, prep["mem_norm"], prep["w_mem_kv"], gate,
                 u, pool_inv, x, prep["w_pool"], prep["pool_scale"], prep["w_out"], prep["norm_post"])
    return x


def _both_trunks(bounded_scores, x_prompt, x_sample, mem_prompt, mem_sample, prep):
    return (_trunk(bounded_scores, x_prompt, mem_prompt, prep),
            _trunk(bounded_scores, x_sample, mem_sample, prep))


def kernel(x_prompt, x_sample, mem_prompt, mem_sample, norm_pre, norm_post, w_in, pool_w, pool_scale,
           q_norm, k_norm, mem_norm, w_mem_kv, w_out):
    n_win = len(POOL_WINDOWS)
    eye = jnp.eye(n_win, dtype=pool_w.dtype)
    base = _rope_base(max(x_prompt.shape[1], x_sample.shape[1]))
    prep = {
        "norm_pre": norm_pre[:, None, :], "norm_post": norm_post[:, None, :],
        "mem_norm": mem_norm[:, None, :], "pool_scale": pool_scale[:, None, :],
        "w_in": w_in.astype(_BF16), "w_out": w_out.astype(_BF16), "w_mem_kv": w_mem_kv.astype(_BF16),
        "w_pool": jnp.einsum("gh,lgcd->lgchd", eye, pool_w).reshape(DEPTH, POOL_W, POOL_W).astype(_BF16),
        "seg": jnp.kron(jnp.eye(MXU_WIDTH // HEAD_DIM, dtype=_F32),
                        jnp.ones((HEAD_DIM, HEAD_DIM), _F32)).astype(_BF16),
        "q_tables": _rope_tables(base, q_norm, Q_SCALE),
        "k_tables": _rope_tables(base, k_norm, 1.0),
    }
    score_bound = (Q_SCALE * HEAD_DIM * BF16_NORM_MARGIN
                   * jnp.max(jnp.abs(q_norm), axis=-1) * jnp.max(jnp.abs(k_norm), axis=-1))
    return lax.cond(jnp.all(score_bound <= BOUNDED_SCORE_LIMIT),
                    functools.partial(_both_trunks, True), functools.partial(_both_trunks, False),
                    x_prompt, x_sample, mem_prompt, mem_sample, prep)
```

```python
import functools
import math

import jax
import jax.numpy as jnp
from jax import lax
from jax.experimental import pallas as pl
from jax.experimental.pallas import tpu as pltpu

D_MODEL = 1024
DEPTH = 2
GRID_W = 64
N_MEM = 256
HEAD_DIM = 64
N_HEADS = 8
N_KV_HEADS = 2
ATTN_W = N_HEADS * HEAD_DIM
KV_W = N_KV_HEADS * HEAD_DIM
POOL_W = 256
POOL_WINDOWS = (2, 4, 8, 16)
POOL_GROUP = POOL_W // len(POOL_WINDOWS)
N_XHEADS = 4
XATTN_W = N_XHEADS * HEAD_DIM
MIX_W = POOL_W + ATTN_W + XATTN_W
IN_W = 2 * POOL_W + 2 * ATTN_W + 2 * KV_W + 2 * XATTN_W
ROPE_THETA = 10000.0
ROPE_PAIRS = HEAD_DIM // 4
EPS = 1e-6
Q_SCALE = math.log2(math.e) * HEAD_DIM ** -0.5
BOUNDED_SCORE_LIMIT = 40.0
BF16_NORM_MARGIN = 1.02

LANES = 128
MXU_WIDTH = 256
BF16_SUBLANES = 16
POOL_HALO = 8
PROJ_ROWS = 1024
PROJ_SUB = 512
KEY_CHUNK = 256
Q_ROWS_CANDIDATES = (1024, 512)
X_SUB = 256
MIX_TEMP_BYTES = 12 * 1024 * 1024
Q_SUB = 256
PAIRS_PER_STREAM = 2
ATTN_LOOKAHEAD = 2
VMEM_LIMIT_BYTES = 56 * 1024 * 1024

VT_ROWS = HEAD_DIM + BF16_SUBLANES

_OFF_U = 0
_OFF_GP = POOL_W
_OFF_Q = 2 * POOL_W
_OFF_K = _OFF_Q + ATTN_W
_OFF_V = _OFF_K + KV_W
_OFF_GA = _OFF_V + KV_W
_OFF_QX = _OFF_GA + ATTN_W
_OFF_GX = _OFF_QX + XATTN_W

_RESIDENT = pl.Buffered(1)

_BF16 = jnp.bfloat16
_F32 = jnp.float32


def _dot(a, b):
    return jnp.dot(a, b, preferred_element_type=_F32)


def _silu(g):
    half = 0.5 * g
    return half + half * jnp.tanh(half)


def _is_even_head_lane(shape):
    return (lax.broadcasted_iota(jnp.int32, shape, len(shape) - 1) % LANES) < HEAD_DIM


def _ones_row_tile(cols):
    row = lax.broadcasted_iota(jnp.int32, (BF16_SUBLANES, cols), 0)
    return jnp.where(row == 0, 1.0, 0.0).astype(_BF16)


def _memory_kv(mem_ref, g_ref, w_ref, k_ref, vt_ref):
    m = mem_ref[0]
    ms = jnp.mean(m * m, axis=-1, keepdims=True)
    mh = (m * lax.rsqrt(ms + EPS) * g_ref[...]).astype(_BF16)
    kv = _dot(mh, w_ref[...])
    k = kv[:, :XATTN_W]
    v_t = kv[:, XATTN_W:].T
    left = _is_even_head_lane((N_MEM, LANES))
    for pair in range(N_XHEADS // 2):
        kp = k[:, pair * LANES:(pair + 1) * LANES]
        k_ref[2 * pair] = jnp.where(left, kp, 0.0).astype(_BF16)
        k_ref[2 * pair + 1] = jnp.where(left, 0.0, kp).astype(_BF16)
    for head in range(N_XHEADS):
        vt_ref[head, 0:HEAD_DIM] = v_t[head * HEAD_DIM:(head + 1) * HEAD_DIM].astype(_BF16)
        vt_ref[head, HEAD_DIM:VT_ROWS] = _ones_row_tile(N_MEM)


def _swap16(x):
    n = x.shape[-1]
    lane = lax.broadcasted_iota(jnp.int32, x.shape, 1)
    first = (lane & ROPE_PAIRS) == 0
    return jnp.where(first, pltpu.roll(x, n - ROPE_PAIRS, 1), pltpu.roll(x, ROPE_PAIRS, 1))


def _norm_rope(z, seg, cos_tab, sin_tab):
    reps = z.shape[1] // LANES
    cos_tab = jnp.concatenate([cos_tab] * reps, axis=1)
    sin_tab = jnp.concatenate([sin_tab] * reps, axis=1)
    ssq = _dot((z * z).astype(_BF16), seg)
    r = lax.rsqrt(ssq * (1.0 / HEAD_DIM) + EPS)
    return r * (z * cos_tab + _swap16(z) * sin_tab)


def _proj_kernel(x_ref, g_ref, w_ref, seg_ref, cq_ref, sq_ref, ck_ref, sk_ref,
                 u_ref, gate_ref, qt_ref, qxt_ref, k_ref, vt_ref):
    seg = seg_ref[...]
    seg_w = seg.shape[0]
    for r0 in range(0, x_ref.shape[1], PROJ_SUB):
        rs = slice(r0, r0 + PROJ_SUB)
        x = x_ref[0, rs, :]
        ms = jnp.mean(x * x, axis=-1, keepdims=True)
        h = (x * lax.rsqrt(ms + EPS) * g_ref[...]).astype(_BF16)

        z_qkv = _dot(h, w_ref[:, _OFF_Q:_OFF_GA])
        z_pool = _dot(h, w_ref[:, _OFF_U:_OFF_Q])

        cq, sq = cq_ref[rs, :], sq_ref[rs, :]
        for lo in range(0, ATTN_W, seg_w):
            q = _norm_rope(z_qkv[:, lo:lo + seg_w], seg, cq, sq)
            for pair in range(lo // LANES, (lo + seg_w) // LANES):
                qt_ref[0, pair, :, rs] = q[:, pair * LANES - lo:(pair + 1) * LANES - lo].T.astype(_BF16)

        k = _norm_rope(z_qkv[:, ATTN_W:ATTN_W + KV_W], seg[:KV_W, :KV_W],
                       ck_ref[rs, :], sk_ref[rs, :])
        k_sw = pltpu.roll(k, HEAD_DIM, 1)
        left = _is_even_head_lane(k.shape)
        k_ref[0, 0, rs, :] = jnp.where(left, k, 0.0).astype(_BF16)
        k_ref[0, 1, rs, :] = jnp.where(left, 0.0, k_sw).astype(_BF16)
        k_ref[0, 2, rs, :] = jnp.where(left, k_sw, 0.0).astype(_BF16)
        k_ref[0, 3, rs, :] = jnp.where(left, 0.0, k).astype(_BF16)

        v_t = z_qkv[:, ATTN_W + KV_W:].T
        for head in range(N_KV_HEADS):
            vt_ref[0, head, 0:HEAD_DIM, rs] = v_t[head * HEAD_DIM:(head + 1) * HEAD_DIM].astype(_BF16)
            vt_ref[0, head, HEAD_DIM:VT_ROWS, rs] = _ones_row_tile(PROJ_SUB)

        u_ref[0, rs, :] = z_pool[:, 0:POOL_W]
        gate_ref[0, rs, 0:POOL_W] = _silu(z_pool[:, POOL_W:]).astype(_BF16)

        z_rest = _dot(h, w_ref[:, _OFF_GA:])
        gate_ref[0, rs, POOL_W:POOL_W + ATTN_W] = _silu(z_rest[:, 0:ATTN_W]).astype(_BF16)
        gate_ref[0, rs, POOL_W + ATTN_W:MIX_W] = _silu(z_rest[:, ATTN_W + XATTN_W:]).astype(_BF16)
        for pair in range(N_XHEADS // 2):
            lo = ATTN_W + pair * LANES
            qxt_ref[0, pair, :, rs] = (z_rest[:, lo:lo + LANES] * Q_SCALE).T.astype(_BF16)


def _proj(layer, x, g, w, seg, cq, sq, ck, sk):
    b, t, _ = x.shape
    rows = PROJ_ROWS
    assert t % rows == 0 and rows % PROJ_SUB == 0 and t % GRID_W == 0, (t, rows)
    n = t // rows
    tab = pl.BlockSpec((None, rows, LANES), lambda i, j: (layer, j, 0))
    return pl.pallas_call(
        _proj_kernel,
        grid=(b, n),
        in_specs=[
            pl.BlockSpec((1, rows, D_MODEL), lambda i, j: (i, j, 0)),
            pl.BlockSpec((None, 1, D_MODEL), lambda i, j: (layer, 0, 0)),
            pl.BlockSpec((None, D_MODEL, IN_W), lambda i, j: (layer, 0, 0), pipeline_mode=_RESIDENT),
            pl.BlockSpec((MXU_WIDTH, MXU_WIDTH), lambda i, j: (0, 0), pipeline_mode=_RESIDENT),
            tab, tab, tab, tab,
        ],
        out_specs=[
            pl.BlockSpec((1, rows, POOL_W), lambda i, j: (i, j, 0)),
            pl.BlockSpec((1, rows, MIX_W), lambda i, j: (i, j, 0)),
            pl.BlockSpec((1, N_HEADS // 2, LANES, rows), lambda i, j: (i, 0, 0, j)),
            pl.BlockSpec((1, N_XHEADS // 2, LANES, rows), lambda i, j: (i, 0, 0, j)),
            pl.BlockSpec((1, 2 * N_KV_HEADS, rows, KV_W), lambda i, j: (i, 0, j, 0)),
            pl.BlockSpec((1, N_KV_HEADS, VT_ROWS, rows), lambda i, j: (i, 0, 0, j)),
        ],
        out_shape=[
            jax.ShapeDtypeStruct((b, t, POOL_W), _F32),
            jax.ShapeDtypeStruct((b, t, MIX_W), _BF16),
            jax.ShapeDtypeStruct((b, N_HEADS // 2, LANES, t), _BF16),
            jax.ShapeDtypeStruct((b, N_XHEADS // 2, LANES, t), _BF16),
            jax.ShapeDtypeStruct((b, 2 * N_KV_HEADS, t, KV_W), _BF16),
            jax.ShapeDtypeStruct((b, N_KV_HEADS, VT_ROWS, t), _BF16),
        ],
        compiler_params=pltpu.CompilerParams(
            dimension_semantics=("arbitrary", "arbitrary"), vmem_limit_bytes=VMEM_LIMIT_BYTES),
        name="proj",
    )(x, g, w, seg, cq, sq, ck, sk)


def _attend(streams, lookahead):
    n_chunks = [s[1].shape[1] // s[5] for s in streams]
    units = [(c, i, parity) for c in range(max(n_chunks)) for i in range(len(streams))
             if c < n_chunks[i] for parity in range(2)]
    units = [u for u in units if n_chunks[u[1]] > 1] + [u for u in units if n_chunks[u[1]] == 1]
    order = []
    for n, unit in enumerate(units):
        order.append(("scores", unit))
        if n >= lookahead:
            order.append(("finish", units[n - lookahead]))
    order += [("finish", unit) for unit in units[max(len(units) - lookahead, 0):]]
    state = [[None, None] for _ in streams]
    scores = {}

    def issue_scores(c, i, parity):
        q_t, k_ref, k_slot = streams[i][:3]
        chunk = streams[i][5]
        return _dot(k_ref[k_slot + parity, c * chunk:(c + 1) * chunk, :], q_t)

    def finish(c, i, parity, s):
        _, _, _, vt_ref, v_slots, chunk, sink, bounded = streams[i]
        values = vt_ref[v_slots[parity], :, c * chunk:(c + 1) * chunk]
        if bounded:
            pv = _dot(values, jnp.exp2(s).astype(_BF16))
            if state[i][parity] is not None:
                pv = state[i][parity][1] + pv
            mn = None
        else:
            mn = jnp.max(s, axis=0, keepdims=True)
            if state[i][parity] is not None:
                m, acc = state[i][parity]
                mn = jnp.maximum(m, mn)
            pv = _dot(values, jnp.exp2((s - mn).astype(_BF16)))
            if state[i][parity] is not None:
                pv = acc * jnp.exp2(m - mn) + pv
        state[i][parity] = (mn, pv)
        if c == n_chunks[i] - 1 and parity == 1:
            out_t = [acc[0:HEAD_DIM] * (1.0 / acc[HEAD_DIM:HEAD_DIM + 1]) for _, acc in state[i]]
            sink(jnp.concatenate(out_t, axis=0).T)

    for what, unit in order:
        if what == "scores":
            scores[unit] = issue_scores(*unit)
        else:
            finish(*unit, scores.pop(unit))


def _pool_delta(u_ref, up_ref, un_ref, inv_ref):
    tq = u_ref.shape[1]
    n_ext = tq + 2 * POOL_HALO
    is_first = pl.program_id(1) == 0
    is_last = pl.program_id(1) == pl.num_programs(1) - 1
    left = _is_even_head_lane((tq, LANES))
    deltas = []
    for half in range(POOL_W // LANES):
        w_a, w_b = POOL_WINDOWS[2 * half], POOL_WINDOWS[2 * half + 1]
        cols = slice(half * LANES, (half + 1) * LANES)
        u = u_ref[0, :, cols]
        ext = jnp.concatenate([jnp.where(is_first, 0.0, up_ref[0, :, cols]), u,
                               jnp.where(is_last, 0.0, un_ref[0, :, cols])], axis=0)
        sums = {}
        run = ext
        for w in POOL_WINDOWS[:2 * half + 2]:
            run = run + pltpu.roll(run, n_ext - w // 2, 0)
            if w in (w_a, w_b):
                aligned = run if w // 2 == POOL_HALO else pltpu.roll(run, w // 2, 0)
                first = 0 if w // 2 == POOL_HALO else POOL_HALO
                sums[w] = aligned[first:first + tq]
        mean = jnp.where(left, sums[w_a], sums[w_b]) * inv_ref[:, cols]
        deltas.append(mean - u)
    return jnp.concatenate(deltas, axis=1)


def _mix_kernel(qt_ref, qxt_ref, k_ref, vt_ref, mem_ref, gmem_ref, wmem_ref, gate_ref, u_ref, up_ref,
                un_ref, inv_ref, x_ref, wpool_ref, pscale_ref, wout_ref, gpost_ref, y_ref,
                mix_ref, km_ref, vmt_ref, *, bounded_scores):
    @pl.when(pl.program_id(1) == 0)
    def _():
        _memory_kv(mem_ref, gmem_ref, wmem_ref, km_ref, vmt_ref)

    d = _pool_delta(u_ref, up_ref, un_ref, inv_ref).astype(_BF16)
    pool = _dot(d, wpool_ref[...]) * pscale_ref[...]
    mix_ref[:, 0:POOL_W] = (pool * gate_ref[0, :, 0:POOL_W].astype(_F32)).astype(_BF16)

    tq = qt_ref.shape[3]
    pairs_per_group = N_HEADS // N_KV_HEADS // 2

    def gated_sink(r0, rows, first_cols):
        def sink(out):
            for n, c0 in enumerate(first_cols):
                gate = gate_ref[0, r0:r0 + rows, c0:c0 + LANES].astype(_F32)
                mix_ref[r0:r0 + rows, c0:c0 + LANES] = (out[n * rows:(n + 1) * rows] * gate).astype(_BF16)
        return sink

    streams = []
    for r0 in range(0, tq, X_SUB):
        for pair in range(N_XHEADS // 2):
            streams.append((qxt_ref[0, pair, :, r0:r0 + X_SUB], km_ref, 2 * pair,
                            vmt_ref, (2 * pair, 2 * pair + 1), N_MEM,
                            gated_sink(r0, X_SUB, [POOL_W + ATTN_W + pair * LANES]), False))
    for r0 in range(0, tq, Q_SUB):
        cols = slice(r0, r0 + Q_SUB)
        for first in range(0, N_HEADS // 2, PAIRS_PER_STREAM):
            group = first // pairs_per_group
            pairs = list(range(first, first + PAIRS_PER_STREAM))
            q_t = jnp.concatenate([qt_ref[0, pair, :, cols] for pair in pairs], axis=1)
            streams.append((q_t, k_ref.at[0], 2 * group, vt_ref.at[0], (group, group), KEY_CHUNK,
                            gated_sink(r0, Q_SUB, [POOL_W + pair * LANES for pair in pairs]),
                            bounded_scores))

    _attend(streams, ATTN_LOOKAHEAD)

    y = _dot(mix_ref[...], wout_ref[...])
    ms = jnp.mean(y * y, axis=-1, keepdims=True)
    y_ref[0] = x_ref[0] + y * lax.rsqrt(ms + EPS) * gpost_ref[...]


def _query_rows(t):
    f32, bf16, buffers = 4, 2, 2
    resident = buffers * bf16 * t * (2 * N_KV_HEADS * KV_W + N_KV_HEADS * VT_ROWS)
    weights = buffers * (bf16 * (MIX_W * D_MODEL + D_MODEL * 2 * XATTN_W + POOL_W * POOL_W)
                         + f32 * N_MEM * D_MODEL)
    per_row = (buffers * (2 * f32 * D_MODEL + bf16 * (MIX_W + ATTN_W + XATTN_W) + 2 * f32 * POOL_W)
               + bf16 * MIX_W)
    for rows in Q_ROWS_CANDIDATES:
        if t % rows == 0 and resident + weights + rows * per_row + MIX_TEMP_BYTES <= VMEM_LIMIT_BYTES:
            return rows
    raise ValueError(f"no query block size fits VMEM for sequence length {t}")


def _mix(layer, bounded_scores, qt, qxt, k, vt, mem, gmem, wmem, gate, u, pool_inv, x, wpool, pscale,
         wout, gpost):
    b, t, _ = x.shape
    tq = _query_rows(t)
    assert t % tq == 0 and tq % Q_SUB == 0 and tq % X_SUB == 0 and t % KEY_CHUNK == 0, (t, tq)
    n = t // tq
    halo_per_q = tq // POOL_HALO
    n_halo = t // POOL_HALO
    return pl.pallas_call(
        functools.partial(_mix_kernel, bounded_scores=bounded_scores),
        grid=(b, n),
        in_specs=[
            pl.BlockSpec((1, N_HEADS // 2, LANES, tq), lambda i, j: (i, 0, 0, j)),
            pl.BlockSpec((1, N_XHEADS // 2, LANES, tq), lambda i, j: (i, 0, 0, j)),
            pl.BlockSpec((1, 2 * N_KV_HEADS, t, KV_W), lambda i, j: (i, 0, 0, 0)),
            pl.BlockSpec((1, N_KV_HEADS, VT_ROWS, t), lambda i, j: (i, 0, 0, 0)),
            pl.BlockSpec((1, N_MEM, D_MODEL), lambda i, j: (i, 0, 0)),
            pl.BlockSpec((None, 1, D_MODEL), lambda i, j: (layer, 0, 0)),
            pl.BlockSpec((None, D_MODEL, 2 * XATTN_W), lambda i, j: (layer, 0, 0), pipeline_mode=_RESIDENT),
            pl.BlockSpec((1, tq, MIX_W), lambda i, j: (i, j, 0)),
            pl.BlockSpec((1, tq, POOL_W), lambda i, j: (i, j, 0)),
            pl.BlockSpec((1, POOL_HALO, POOL_W),
                         lambda i, j: (i, jnp.maximum(j * halo_per_q - 1, 0), 0)),
            pl.BlockSpec((1, POOL_HALO, POOL_W),
                         lambda i, j: (i, jnp.minimum((j + 1) * halo_per_q, n_halo - 1), 0)),
            pl.BlockSpec((tq, POOL_W), lambda i, j: (j, 0)),
            pl.BlockSpec((1, tq, D_MODEL), lambda i, j: (i, j, 0)),
            pl.BlockSpec((None, POOL_W, POOL_W), lambda i, j: (layer, 0, 0), pipeline_mode=_RESIDENT),
            pl.BlockSpec((None, 1, POOL_W), lambda i, j: (layer, 0, 0)),
            pl.BlockSpec((None, MIX_W, D_MODEL), lambda i, j: (layer, 0, 0), pipeline_mode=_RESIDENT),
            pl.BlockSpec((None, 1, D_MODEL), lambda i, j: (layer, 0, 0)),
        ],
        out_specs=pl.BlockSpec((1, tq, D_MODEL), lambda i, j: (i, j, 0)),
        out_shape=jax.ShapeDtypeStruct((b, t, D_MODEL), _F32),
        scratch_shapes=[
            pltpu.VMEM((tq, MIX_W), _BF16),
            pltpu.VMEM((N_XHEADS, N_MEM, LANES), _BF16),
            pltpu.VMEM((N_XHEADS, VT_ROWS, N_MEM), _BF16),
        ],
        compiler_params=pltpu.CompilerParams(
            dimension_semantics=("arbitrary", "arbitrary"), vmem_limit_bytes=VMEM_LIMIT_BYTES),
        name="mix",
    )(qt, qxt, k, vt, mem, gmem, wmem, gate, u, u, u, pool_inv, x, wpool, pscale, wout, gpost)


def _rope_base(t):
    rows = t // GRID_W
    row = jnp.repeat(jnp.arange(rows), GRID_W).astype(_F32)
    col = jnp.tile(jnp.arange(GRID_W), rows).astype(_F32)
    freqs = ROPE_THETA ** (-jnp.arange(ROPE_PAIRS, dtype=_F32) / ROPE_PAIRS)
    ang = jnp.stack([row[:, None] * freqs, col[:, None] * freqs], axis=1)
    cos, sin = jnp.cos(ang), jnp.sin(ang)
    cos_h = jnp.stack([cos, cos], axis=2).reshape(t, HEAD_DIM)
    sin_h = jnp.stack([-sin, sin], axis=2).reshape(t, HEAD_DIM)
    return cos_h, sin_h


def _rope_tables(base, gains, scale):
    cos_h, sin_h = base
    g = gains.astype(_F32)
    g_sw = g.reshape(-1, 2, 2, ROPE_PAIRS)[:, :, ::-1, :].reshape(-1, HEAD_DIM)
    cos_t = cos_h[None] * (g * scale)[:, None, :]
    sin_t = sin_h[None] * (g_sw * scale)[:, None, :]
    reps = (1, 1, LANES // HEAD_DIM)
    return jnp.tile(cos_t, reps), jnp.tile(sin_t, reps)


def _pool_inv_counts(t):
    pos = jnp.arange(t)[:, None]
    w = jnp.repeat(jnp.array(POOL_WINDOWS), POOL_GROUP)[None, :]
    cnt = jnp.minimum(pos + (w - w // 2), t) - jnp.maximum(pos - w // 2, 0)
    return 1.0 / cnt.astype(_F32)


def _trunk(bounded_scores, x, mem, prep):
    pool_inv = _pool_inv_counts(x.shape[1])
    for layer in range(DEPTH):
        u, gate, qt, qxt, k, vt = _proj(layer, x, prep["norm_pre"], prep["w_in"], prep["seg"],
                                        *prep["q_tables"], *prep["k_tables"])
        x = _mix(layer, bounded_scores, qt, qxt, k, vt, mem, prep["mem_norm"], prep["w_mem_kv"], gate,
                 u, pool_inv, x, prep["w_pool"], prep["pool_scale"], prep["w_out"], prep["norm_post"])
    return x


def _both_trunks(bounded_scores, x_prompt, x_sample, mem_prompt, mem_sample, prep):
    return (_trunk(bounded_scores, x_prompt, mem_prompt, prep),
            _trunk(bounded_scores, x_sample, mem_sample, prep))


def kernel(x_prompt, x_sample, mem_prompt, mem_sample, norm_pre, norm_post, w_in, pool_w, pool_scale,
           q_norm, k_norm, mem_norm, w_mem_kv, w_out):
    n_win = len(POOL_WINDOWS)
    eye = jnp.eye(n_win, dtype=pool_w.dtype)
    base = _rope_base(max(x_prompt.shape[1], x_sample.shape[1]))
    prep = {
        "norm_pre": norm_pre[:, None, :], "norm_post": norm_post[:, None, :],
        "mem_norm": mem_norm[:, None, :], "pool_scale": pool_scale[:, None, :],
        "w_in": w_in.astype(_BF16), "w_out": w_out.astype(_BF16), "w_mem_kv": w_mem_kv.astype(_BF16),
        "w_pool": jnp.einsum("gh,lgcd->lgchd", eye, pool_w).reshape(DEPTH, POOL_W, POOL_W).astype(_BF16),
        "seg": jnp.kron(jnp.eye(MXU_WIDTH // HEAD_DIM, dtype=_F32),
                        jnp.ones((HEAD_DIM, HEAD_DIM), _F32)).astype(_BF16),
        "q_tables": _rope_tables(base, q_norm, Q_SCALE),
        "k_tables": _rope_tables(base, k_norm, 1.0),
    }
    score_bound = (Q_SCALE * HEAD_DIM * BF16_NORM_MARGIN
                   * jnp.max(jnp.abs(q_norm), axis=-1) * jnp.max(jnp.abs(k_norm), axis=-1))
    return lax.cond(jnp.all(score_bound <= BOUNDED_SCORE_LIMIT),
                    functools.partial(_both_trunks, True), functools.partial(_both_trunks, False),
                    x_prompt, x_sample, mem_prompt, mem_sample, prep)
```
